```python
import jax, jax.numpy as jnp
from jax import lax
import numpy as np

D_MODEL = 2048
BATCH = 2
SEQ = 4096
DEPTH = 1
DEC_BATCH = 128
DEC_SEQ = 4
PAST_LEN = 16384
PAGE_SIZE = 128

MIX_WIDTH = D_MODEL
MLA_WIDTH = MIX_WIDTH // 2
CMLP_WIDTH = MIX_WIDTH - MLA_WIDTH
V_HEAD = 128
MLA_HEADS = MLA_WIDTH // V_HEAD
QK_NOPE = 128
QK_ROPE = 64
Q_RANK = 512
KV_RANK = 256
CHUNK = 128
CMLP_GW = 128
CMLP_GROUPS = CMLP_WIDTH // CMLP_GW
D_IN = Q_RANK + KV_RANK + QK_ROPE + 2 * CMLP_WIDTH
MEM_LEN = 256
MEM_HEADS = 4
MEM_HD = 128
MEM_INNER = MEM_HEADS * MEM_HD
D_FF = 4 * D_MODEL
Q_BLOCK = 128
ROPE_THETA = 10000.0
EPS = 1e-6
MLA_SCALE = (QK_NOPE + QK_ROPE) ** -0.5

kernel_name = 'hymba_mla_chunkmlp_decoder_step'


def rms_norm(x, g):
    xf = x.astype(jnp.float32)
    y = xf * lax.rsqrt(jnp.mean(xf * xf, axis=-1, keepdims=True) + EPS)
    return (y * g.astype(jnp.float32)).astype(x.dtype)


def apply_rope(x, pos):
    half = QK_ROPE // 2
    inv = ROPE_THETA ** (-jnp.arange(half, dtype=jnp.float32) / half)
    ang = pos.astype(jnp.float32)[:, None] * inv[None, :]
    shp = (ang.shape[0],) + (1,) * (x.ndim - 3) + (half,)
    cos = jnp.cos(ang).reshape(shp)
    sin = jnp.sin(ang).reshape(shp)
    xf = x.astype(jnp.float32)
    x1, x2 = xf[..., :half], xf[..., half:]
    return jnp.concatenate([x1 * cos - x2 * sin, x1 * sin + x2 * cos], axis=-1).astype(x.dtype)


def mixer_projections(h, pos, w_in, q_norm_g, w_q_up, kv_norm_g, w_uk, v_norm_g):
    B, S, _ = h.shape
    z = h @ w_in
    o1 = Q_RANK
    o2 = o1 + KV_RANK
    o3 = o2 + QK_ROPE
    o4 = o3 + CMLP_WIDTH
    q = (rms_norm(z[..., :o1], q_norm_g) @ w_q_up).reshape(B, S, MLA_HEADS, QK_NOPE + QK_ROPE)
    q_abs = jnp.einsum('bshd,rhd->bshr', q[..., :QK_NOPE], w_uk)
    q_pe = apply_rope(q[..., QK_NOPE:], pos)
    kv_c = rms_norm(z[..., o1:o2], kv_norm_g)
    k_pe = apply_rope(z[..., o2:o3], pos)
    u = jax.nn.gelu(z[..., o3:o4]).reshape(B, S, CMLP_GROUPS, CMLP_GW)
    v = rms_norm(jax.nn.gelu(z[..., o4:]), v_norm_g).reshape(B, S, CMLP_GROUPS, CMLP_GW)
    return q_abs, q_pe, kv_c, k_pe, u, v


def mla_prompt_attend(q_abs, q_pe, kv_c, k_pe):
    B, S = kv_c.shape[:2]
    key_pos = jnp.arange(S)

    def one_block(i):
        q0 = i * Q_BLOCK
        qa = lax.dynamic_slice_in_dim(q_abs, q0, Q_BLOCK, axis=1)
        qp = lax.dynamic_slice_in_dim(q_pe, q0, Q_BLOCK, axis=1)
        s = (jnp.einsum('bqhr,bkr->bhqk', qa, kv_c)
             + jnp.einsum('bqhd,bkd->bhqk', qp, k_pe)).astype(jnp.float32) * MLA_SCALE
        causal = key_pos[None, :] <= (q0 + jnp.arange(Q_BLOCK))[:, None]
        p = jax.nn.softmax(jnp.where(causal, s, -jnp.inf), axis=-1).astype(kv_c.dtype)
        return jnp.einsum('bhqk,bkr->bqhr', p, kv_c)

    o = lax.map(one_block, jnp.arange(S // Q_BLOCK))
    return o.transpose(1, 0, 2, 3, 4).reshape(B, S, MLA_HEADS, KV_RANK)


def mla_sample_attend(q_abs, q_pe, kv_c, k_pe, past_c, past_pe):
    T = kv_c.shape[1]
    P = past_c.shape[1]
    s_past = jnp.einsum('bqhr,bkr->bhqk', q_abs, past_c) + jnp.einsum('bqhd,bkd->bhqk', q_pe, past_pe)
    s_new = jnp.einsum('bqhr,bkr->bhqk', q_abs, kv_c) + jnp.einsum('bqhd,bkd->bhqk', q_pe, k_pe)
    s = jnp.concatenate([s_past, s_new], axis=-1).astype(jnp.float32) * MLA_SCALE
    visible = jnp.concatenate([jnp.ones((T, P), bool), jnp.tril(jnp.ones((T, T), bool))], axis=1)
    p = jax.nn.softmax(jnp.where(visible, s, -jnp.inf), axis=-1).astype(kv_c.dtype)
    return (jnp.einsum('bhqk,bkr->bqhr', p[..., :P], past_c)
            + jnp.einsum('bhqk,bkr->bqhr', p[..., P:], kv_c))


def chunk_spatial_mix(u, v, w_spatial, b_spatial):
    B, S = v.shape[:2]
    pad = (-S) % CHUNK
    vp = jnp.pad(v, ((0, 0), (0, pad), (0, 0), (0, 0)))
    n = (S + pad) // CHUNK
    vc = vp.reshape(B, n, CHUNK, CMLP_GROUPS, CMLP_GW)
    mask = jnp.tril(jnp.ones((CHUNK, CHUNK), bool))
    w = jnp.where(mask[None], w_spatial, jnp.zeros((), w_spatial.dtype))
    mixed = jnp.einsum('gts,bnsgc->bntgc', w, vc) + b_spatial.T[None, None, :, :, None]
    mixed = mixed.reshape(B, n * CHUNK, CMLP_GROUPS, CMLP_GW)[:, :S]
    return (u * mixed).reshape(B, S, CMLP_WIDTH)


def mixer_merge(o_lat, u, v, w_uv, w_spatial, b_spatial, attn_out_g, cmlp_out_g, w_out):
    B, S = u.shape[:2]
    attn = jnp.einsum('bshr,rhd->bshd', o_lat, w_uv).reshape(B, S, MLA_WIDTH)
    cm = chunk_spatial_mix(u, v, w_spatial, b_spatial)
    return jnp.concatenate([rms_norm(attn, attn_out_g), rms_norm(cm, cmlp_out_g)], axis=-1) @ w_out


def memory_kv(mem, mem_in_g, w_mem_k, w_mem_v):
    B = mem.shape[0]
    m = rms_norm(mem, mem_in_g)
    k = (m @ w_mem_k).reshape(B, MEM_LEN, MEM_HEADS, MEM_HD)
    v = (m @ w_mem_v).reshape(B, MEM_LEN, MEM_HEADS, MEM_HD)
    return k, v


def memory_attend(h, mk, mv, w_mem_q, w_mem_o):
    B, S, _ = h.shape
    q = (h @ w_mem_q).reshape(B, S, MEM_HEADS, MEM_HD)
    s = jnp.einsum('bshd,bmhd->bhsm', q, mk).astype(jnp.float32) * (MEM_HD ** -0.5)
    p = jax.nn.softmax(s, axis=-1).astype(mv.dtype)
    o = jnp.einsum('bhsm,bmhd->bshd', p, mv).reshape(B, S, MEM_INNER)
    return o @ w_mem_o


def layer_tail(x, mix, mk, mv, norm_mem_g, w_mem_q, w_mem_o, norm_ffn_g, w_ffn_up, w_ffn_down):
    x = x + mix
    x = x + memory_attend(rms_norm(x, norm_mem_g), mk, mv, w_mem_q, w_mem_o)
    hf = rms_norm(x, norm_ffn_g)
    return x + jnp.square(jax.nn.relu(hf @ w_ffn_up)) @ w_ffn_down


def setup_inputs(seed: int = 0) -> dict:
    key = jax.random.key(seed)
    keys = jax.random.split(key, 40)
    cnt = [0]

    def nrm(shape, scale=1.0):
        k = keys[cnt[0]]
        cnt[0] += 1
        return jax.random.normal(k, shape, jnp.float32) * scale

    def gain(shape):
        return 1.0 + 0.01 * nrm(shape)

    L = DEPTH
    n_pages = PAST_LEN // PAGE_SIZE
    n_used = DEC_BATCH * n_pages
    n_pool = n_used + (n_used + 3) // 4
    perm = jax.random.permutation(keys[39], n_pool)[:n_used]
    page_table = perm.reshape(DEC_BATCH, n_pages).astype(jnp.int32)
    return {
        'x_prompt': nrm((BATCH, SEQ, D_MODEL)),
        'x_sample': nrm((DEC_BATCH, DEC_SEQ, D_MODEL)),
        'cache_kv_latent': nrm((L, n_pool, PAGE_SIZE, KV_RANK)),
        'cache_k_rope': nrm((L, n_pool, PAGE_SIZE, QK_ROPE)),
        'cache_mem_k': nrm((L, DEC_BATCH, MEM_LEN, MEM_HEADS, MEM_HD)),
        'cache_mem_v': nrm((L, DEC_BATCH, MEM_LEN, MEM_HEADS, MEM_HD)),
        'page_table': page_table,
        'mem_prompt': nrm((BATCH, MEM_LEN, D_MODEL)),
        'norm_mix_g': gain((L, D_MODEL)),
        'w_in': nrm((L, D_MODEL, D_IN), D_MODEL ** -0.5),
        'q_norm_g': gain((L, Q_RANK)),
        'w_q_up': nrm((L, Q_RANK, MLA_HEADS * (QK_NOPE + QK_ROPE)), Q_RANK ** -0.5),
        'kv_norm_g': gain((L, KV_RANK)),
        'w_uk': nrm((L, KV_RANK, MLA_HEADS, QK_NOPE), KV_RANK ** -0.5),
        'w_uv': nrm((L, KV_RANK, MLA_HEADS, V_HEAD), KV_RANK ** -0.5),
        'v_norm_g': gain((L, CMLP_WIDTH)),
        'w_spatial': nrm((L, CMLP_GROUPS, CHUNK, CHUNK), CHUNK ** -0.5),
        'b_spatial': 1.0 + 0.1 * nrm((L, CMLP_GROUPS, CHUNK)),
        'attn_out_g': gain((L, MLA_WIDTH)),
        'cmlp_out_g': gain((L, CMLP_WIDTH)),
        'w_out': nrm((L, MIX_WIDTH, D_MODEL), MIX_WIDTH ** -0.5),
        'norm_mem_g': gain((L, D_MODEL)),
        'mem_in_g': gain((L, D_MODEL)),
        'w_mem_q': nrm((L, D_MODEL, MEM_INNER), D_MODEL ** -0.5),
        'w_mem_k': nrm((L, D_MODEL, MEM_INNER), D_MODEL ** -0.5),
        'w_mem_v': nrm((L, D_MODEL, MEM_INNER), D_MODEL ** -0.5),
        'w_mem_o': nrm((L, MEM_INNER, D_MODEL), MEM_INNER ** -0.5),
        'norm_ffn_g': gain((L, D_MODEL)),
        'w_ffn_up': nrm((L, D_MODEL, D_FF), D_MODEL ** -0.5),
        'w_ffn_down': nrm((L, D_FF, D_MODEL), D_FF ** -0.5),
        'final_norm_g': gain((D_MODEL,)),
    }


def reference(x_prompt, x_sample, cache_kv_latent, cache_k_rope, cache_mem_k, cache_mem_v,
              page_table, mem_prompt, norm_mix_g, w_in, q_norm_g, w_q_up, kv_norm_g, w_uk, w_uv,
              v_norm_g, w_spatial, b_spatial, attn_out_g, cmlp_out_g, w_out, norm_mem_g, mem_in_g,
              w_mem_q, w_mem_k, w_mem_v, w_mem_o, norm_ffn_g, w_ffn_up, w_ffn_down, final_norm_g):
    S = x_prompt.shape[1]
    DB, T = x_sample.shape[:2]
    pos_p = jnp.arange(S)
    pos_s = PAST_LEN + jnp.arange(T)
    past_len = page_table.shape[1] * cache_kv_latent.shape[2]
    xp, xs = x_prompt, x_sample
    c_p, pe_p, mk_p, mv_p, c_s, pe_s, v_s = [], [], [], [], [], [], []
    for l in range(DEPTH):
        hp = rms_norm(xp, norm_mix_g[l])
        q_abs, q_pe, kv_c, k_pe, u, v = mixer_projections(
            hp, pos_p, w_in[l], q_norm_g[l], w_q_up[l], kv_norm_g[l], w_uk[l], v_norm_g[l])
        o_lat = mla_prompt_attend(q_abs, q_pe, kv_c, k_pe)
        mix = mixer_merge(o_lat, u, v, w_uv[l], w_spatial[l], b_spatial[l],
                          attn_out_g[l], cmlp_out_g[l], w_out[l])
        mk, mv = memory_kv(mem_prompt, mem_in_g[l], w_mem_k[l], w_mem_v[l])
        xp = layer_tail(xp, mix, mk, mv, norm_mem_g[l], w_mem_q[l], w_mem_o[l],
                        norm_ffn_g[l], w_ffn_up[l], w_ffn_down[l])
        c_p.append(kv_c)
        pe_p.append(k_pe)
        mk_p.append(mk)
        mv_p.append(mv)

        hs = rms_norm(xs, norm_mix_g[l])
        q_abs_s, q_pe_s, kv_c_s, k_pe_s, u_s, vv_s = mixer_projections(
            hs, pos_s, w_in[l], q_norm_g[l], w_q_up[l], kv_norm_g[l], w_uk[l], v_norm_g[l])
        past_c = cache_kv_latent[l, page_table].reshape(DB, past_len, KV_RANK)
        past_pe = cache_k_rope[l, page_table].reshape(DB, past_len, QK_ROPE)
        o_lat_s = mla_sample_attend(q_abs_s, q_pe_s, kv_c_s, k_pe_s, past_c, past_pe)
        mix_s = mixer_merge(o_lat_s, u_s, vv_s, w_uv[l], w_spatial[l], b_spatial[l],
                            attn_out_g[l], cmlp_out_g[l], w_out[l])
        xs = layer_tail(xs, mix_s, cache_mem_k[l], cache_mem_v[l], norm_mem_g[l], w_mem_q[l],
                        w_mem_o[l], norm_ffn_g[l], w_ffn_up[l], w_ffn_down[l])
        c_s.append(kv_c_s)
        pe_s.append(k_pe_s)
        v_s.append(vv_s)

    y_prompt = rms_norm(xp, final_norm_g)
    y_sample = rms_norm(xs, final_norm_g)
    return (y_prompt, y_sample, jnp.stack(c_p), jnp.stack(pe_p), jnp.stack(mk_p), jnp.stack(mv_p),
            jnp.stack(c_s), jnp.stack(pe_s), jnp.stack(v_s))
```

```python
import functools

import jax
import jax.numpy as jnp
from jax import lax
from jax.experimental import pallas as pl
from jax.experimental.pallas import tpu as pltpu

F32 = jnp.float32
BF16 = jnp.bfloat16

EPS = 1e-6
ROPE_THETA = 10000.0
PAST_LEN = 16384

LANES = 128
V7X_VMEM_BYTES = 64 * 2**20
VMEM_LIMIT = V7X_VMEM_BYTES - 8 * 2**20

Q_RANK = 512
KV_RANK = 256
QK_NOPE = 128
QK_ROPE = 64
HEADS = 8
V_HEAD = 128
GROUPS = 8
GW = 128
CHUNK = 128
CW = GROUPS * GW
AW = HEADS * V_HEAD
KW = KV_RANK + LANES
MEM_HEADS = 4
MEM_HD = 128
MEM_INNER = MEM_HEADS * MEM_HD
MLA_SCALE = (QK_NOPE + QK_ROPE) ** -0.5
MEM_SCALE = MEM_HD ** -0.5

O_Q = 0
O_KV = O_Q + Q_RANK
O_PE = O_KV + KV_RANK
O_U = O_PE + LANES
O_V = O_U + CW
D_IN_PAD = O_V + CW

NT_DIMS = (((1,), (1,)), ((), ()))


def _const_spec(shape):
    return pl.BlockSpec(shape, lambda *_: (0,) * len(shape), pipeline_mode=pl.Buffered(1))


def _params(semantics):
    return pltpu.CompilerParams(dimension_semantics=semantics, vmem_limit_bytes=VMEM_LIMIT)


def _rms(x, g):
    return x * lax.rsqrt(jnp.mean(x * x, axis=-1, keepdims=True) + EPS) * g


def _gelu(x):
    return 0.5 * x * (1.0 + jnp.tanh(0.7978845608028654 * (x + 0.044715 * (x * x * x))))


def _dot(a, b):
    return jnp.dot(a, b, preferred_element_type=F32)


def _dot_nt(a, b):
    return lax.dot_general(a, b, NT_DIMS, preferred_element_type=F32)


def _proj_kernel(x_ref, gmix_ref, win_ref, qg_ref, wq_ref, kvg_ref, wuk_ref, vg_ref,
                 cos_ref, s1_ref, s2_ref, wmix_ref, bmix_ref, cmg_ref,
                 q_ref, kmat_ref, kvc_ref, kpe_ref, cm_ref, *rest, emit_v):
    if emit_v:
        v_ref, cm_sc = rest
    else:
        (cm_sc,) = rest
    tm = x_ref.shape[0]
    h = _rms(x_ref[...], gmix_ref[...]).astype(BF16)

    def seg(lo, hi):
        return _dot(h, win_ref[:, lo:hi])

    cosv, s1, s2 = cos_ref[...], s1_ref[...], s2_ref[...]

    def rope(p):
        return (p * cosv + pltpu.roll(p, LANES - QK_ROPE // 2, 1) * s1
                + pltpu.roll(p, QK_ROPE // 2, 1) * s2)

    qc = _rms(seg(O_Q, O_KV), qg_ref[...]).astype(BF16)
    q = _dot(qc, wq_ref[...])
    for hh in range(HEADS):
        base = hh * 2 * LANES
        qa = _dot(q[:, base:base + QK_NOPE].astype(BF16), wuk_ref[hh])
        q_ref[hh, :, 0:KV_RANK] = (qa * MLA_SCALE).astype(BF16)
        q_ref[hh, :, KV_RANK:KW] = (rope(q[:, base + LANES:base + 2 * LANES]) * MLA_SCALE).astype(BF16)

    kv = _rms(seg(O_KV, O_PE), kvg_ref[...])
    kvc_ref[...] = kv
    kmat_ref[:, 0:KV_RANK] = kv.astype(BF16)
    kp = rope(seg(O_PE, O_U))
    kpe_ref[...] = kp[:, 0:QK_ROPE]
    kmat_ref[:, KV_RANK:KW] = kp.astype(BF16)

    u = _gelu(seg(O_U, O_V))
    v = _rms(_gelu(seg(O_V, D_IN_PAD)), vg_ref[...])
    if emit_v:
        v_ref[...] = v
    vb = v.astype(BF16)
    row = lax.broadcasted_iota(jnp.int32, (CHUNK, CHUNK), 0)
    col = lax.broadcasted_iota(jnp.int32, (CHUNK, CHUNK), 1)
    for g in range(GROUPS):
        wg = jnp.where(row >= col, wmix_ref[g], 0.0).astype(BF16)
        bg = bmix_ref[:, g:g + 1]
        for c in range(tm // CHUNK):
            rs = slice(c * CHUNK, (c + 1) * CHUNK)
            cs = slice(g * GW, (g + 1) * GW)
            cm_sc[rs, cs] = u[rs, cs] * (_dot(wg, vb[rs, cs]) + bg)
    cm_ref[...] = _rms(cm_sc[...], cmg_ref[...]).astype(BF16)


def _proj(x, tabs, tab_blocks, wmix, bmix, w, *, tm, emit_v):
    rows, d = x.shape
    n = rows // tm
    cos_t, s1_t, s2_t = tabs
    tab_spec = pl.BlockSpec((tm, LANES), lambda i: (i % tab_blocks, 0))
    row_spec = lambda width: pl.BlockSpec((tm, width), lambda i: (i, 0))
    in_specs = [
        row_spec(d), _const_spec((1, d)), _const_spec((d, D_IN_PAD)), _const_spec((1, Q_RANK)),
        _const_spec((Q_RANK, HEADS * 2 * LANES)), _const_spec((1, KV_RANK)),
        _const_spec((HEADS, QK_NOPE, KV_RANK)), _const_spec((1, CW)),
        tab_spec, tab_spec, tab_spec,
        _const_spec((GROUPS, CHUNK, CHUNK)), _const_spec((CHUNK, GROUPS)), _const_spec((1, CW)),
    ]
    out_shape = [
        jax.ShapeDtypeStruct((HEADS, rows, KW), BF16),
        jax.ShapeDtypeStruct((rows, KW), BF16),
        jax.ShapeDtypeStruct((rows, KV_RANK), F32),
        jax.ShapeDtypeStruct((rows, QK_ROPE), F32),
        jax.ShapeDtypeStruct((rows, CW), BF16),
    ]
    out_specs = [
        pl.BlockSpec((HEADS, tm, KW), lambda i: (0, i, 0)),
        row_spec(KW), row_spec(KV_RANK), row_spec(QK_ROPE), row_spec(CW),
    ]
    if emit_v:
        out_shape.append(jax.ShapeDtypeStruct((rows, CW), F32))
        out_specs.append(row_spec(CW))
    return pl.pallas_call(
        functools.partial(_proj_kernel, emit_v=emit_v),
        grid=(n,), in_specs=in_specs, out_specs=out_specs, out_shape=out_shape,
        scratch_shapes=[pltpu.VMEM((tm, CW), F32)],
        compiler_params=_params(("arbitrary",)),
        name="proj_v" if emit_v else "proj",
    )(x, w["gmix"], w["win"], w["qg"], w["wq"], w["kvg"], w["wuk"], w["vg"],
      cos_t, s1_t, s2_t, wmix, bmix, w["cmg"])


def _memkv_kernel(m_ref, g_ref, wk_ref, wv_ref, k_ref, v_ref):
    m = _rms(m_ref[...], g_ref[...]).astype(BF16)
    k_ref[...] = _dot(m, wk_ref[...])
    v_ref[...] = _dot(m, wv_ref[...])


def _memkv(mem, g, wk, wv, *, tm):
    rows, d = mem.shape
    row_spec = lambda width: pl.BlockSpec((tm, width), lambda i: (i, 0))
    return pl.pallas_call(
        _memkv_kernel, grid=(rows // tm,),
        in_specs=[row_spec(d), _const_spec((1, d)), _const_spec((d, MEM_INNER)),
                  _const_spec((d, MEM_INNER))],
        out_specs=[row_spec(MEM_INNER), row_spec(MEM_INNER)],
        out_shape=[jax.ShapeDtypeStruct((rows, MEM_INNER), F32)] * 2,
        compiler_params=_params(("arbitrary",)), name="memkv",
    )(mem, g, wk, wv)


def _online_update(s, v, m_sc, l_sc, acc_sc):
    m_prev = m_sc[...]
    m_new = jnp.maximum(m_prev, jnp.max(s, axis=-1, keepdims=True))
    p = jnp.exp(s - m_new)
    alpha = jnp.exp(m_prev - m_new)
    l_sc[...] = alpha * l_sc[...] + jnp.sum(p, axis=-1, keepdims=True)
    acc_sc[...] = alpha * acc_sc[...] + _dot(p.astype(BF16), v)
    m_sc[...] = m_new


def _online_init(m_sc, l_sc, acc_sc):
    m_sc[...] = jnp.full(m_sc.shape, -jnp.inf, F32)
    l_sc[...] = jnp.zeros(l_sc.shape, F32)
    acc_sc[...] = jnp.zeros(acc_sc.shape, F32)


def _pattn_kernel(q_ref, k_ref, o_ref, m_sc, l_sc, acc_sc):
    i = pl.program_id(1)
    heads, tq, kw = q_ref.shape
    rows = heads * tq
    _online_init(m_sc, l_sc, acc_sc)

    def scores(j):
        kj = k_ref[pl.ds(pl.multiple_of(j * tq, tq), tq), :]
        return _dot_nt(q_ref[...].reshape(rows, kw), kj), kj[:, 0:KV_RANK]

    def full_block(j, carry):
        s, vj = scores(j)
        _online_update(s, vj, m_sc, l_sc, acc_sc)
        return carry

    lax.fori_loop(0, i, full_block, 0)

    s, vj = scores(i)
    t = lax.broadcasted_iota(jnp.int32, (rows, tq), 0) % tq
    c = lax.broadcasted_iota(jnp.int32, (rows, tq), 1)
    _online_update(jnp.where(c <= t, s, -jnp.inf), vj, m_sc, l_sc, acc_sc)

    o = acc_sc[...] * (1.0 / l_sc[...])
    o_ref[...] = o.reshape(heads, tq, KV_RANK).astype(BF16)


def _pattn(q, kmat, *, batch, seq, tq):
    heads = q.shape[0]
    nq = seq // tq
    return pl.pallas_call(
        _pattn_kernel, grid=(batch, nq),
        in_specs=[pl.BlockSpec((heads, tq, KW), lambda b, i: (0, b * nq + i, 0)),
                  pl.BlockSpec((seq, KW), lambda b, i: (b, 0))],
        out_specs=pl.BlockSpec((heads, tq, KV_RANK), lambda b, i: (0, b * nq + i, 0)),
        out_shape=jax.ShapeDtypeStruct((heads, batch * seq, KV_RANK), BF16),
        scratch_shapes=[pltpu.VMEM((heads * tq, 1), F32), pltpu.VMEM((heads * tq, 1), F32),
                        pltpu.VMEM((heads * tq, KV_RANK), F32)],
        compiler_params=_params(("arbitrary", "arbitrary")), name="pattn",
    )(q, kmat)


def _dattn_kernel(pt_ref, q_ref, kn_ref, ckv_hbm, cpe_hbm, o_ref,
                  kvbuf, pebuf, sems, m_sc, l_sc, acc_sc, *, layer, pages_per_step, new_tokens):
    b = pl.program_id(0)
    c = pl.program_id(1)
    nb = pl.num_programs(0)
    nc = pl.num_programs(1)
    page = kvbuf.shape[1] // pages_per_step
    n_pages = nc * pages_per_step
    step = b * nc + c
    slot = step % 2

    def page_copies(j, page_id, sl):
        rows = pl.ds(pl.multiple_of(j * page, page), page)
        return (pltpu.make_async_copy(ckv_hbm.at[layer, page_id], kvbuf.at[sl, rows], sems.at[0, sl]),
                pltpu.make_async_copy(cpe_hbm.at[layer, page_id], pebuf.at[sl, rows], sems.at[1, sl]))

    def start_step(bb, cc, sl):
        def body(j, carry):
            for cp in page_copies(j, pt_ref[bb * n_pages + cc * pages_per_step + j], sl):
                cp.start()
            return carry
        lax.fori_loop(0, pages_per_step, body, 0)

    @pl.when(step == 0)
    def _():
        start_step(0, 0, 0)

    @pl.when(step + 1 < nb * nc)
    def _():
        wrap = c + 1 == nc
        start_step(jnp.where(wrap, b + 1, b), jnp.where(wrap, 0, c + 1), 1 - slot)

    @pl.when(c == 0)
    def _():
        _online_init(m_sc, l_sc, acc_sc)

    def wait_body(j, carry):
        for cp in page_copies(j, 0, slot):
            cp.wait()
        return carry
    lax.fori_loop(0, pages_per_step, wait_body, 0)

    q = q_ref[0]
    kv = kvbuf[slot].astype(BF16)
    pe = pebuf[slot].astype(BF16)
    s = _dot_nt(q[:, 0:KV_RANK], kv) + _dot_nt(q[:, KV_RANK:KV_RANK + QK_ROPE], pe)
    _online_update(s, kv, m_sc, l_sc, acc_sc)

    @pl.when(c == nc - 1)
    def _():
        kn = kn_ref[0]
        sn = _dot_nt(q, kn)
        t = lax.broadcasted_iota(jnp.int32, sn.shape, 0) % new_tokens
        cidx = lax.broadcasted_iota(jnp.int32, sn.shape, 1)
        _online_update(jnp.where(cidx <= t, sn, -jnp.inf), kn[:, 0:KV_RANK], m_sc, l_sc, acc_sc)
        o_ref[0] = (acc_sc[...] * (1.0 / l_sc[...])).astype(BF16)


def _dattn(page_table, q, knew, cache_kv, cache_pe, *, layer, pages_per_step, new_tokens):
    nbatch, rows, _ = q.shape
    n_pages = page_table.shape[1]
    page = cache_kv.shape[2]
    nc = n_pages // pages_per_step
    keys = pages_per_step * page
    grid_spec = pltpu.PrefetchScalarGridSpec(
        num_scalar_prefetch=1, grid=(nbatch, nc),
        in_specs=[pl.BlockSpec((1, rows, KW), lambda b, c, pt: (b, 0, 0)),
                  pl.BlockSpec((1, LANES, KW), lambda b, c, pt: (b, 0, 0)),
                  pl.BlockSpec(memory_space=pl.ANY), pl.BlockSpec(memory_space=pl.ANY)],
        out_specs=pl.BlockSpec((1, rows, KV_RANK), lambda b, c, pt: (b, 0, 0)),
        scratch_shapes=[pltpu.VMEM((2, keys, KV_RANK), F32), pltpu.VMEM((2, keys, QK_ROPE), F32),
                        pltpu.SemaphoreType.DMA((2, 2)),
                        pltpu.VMEM((rows, 1), F32), pltpu.VMEM((rows, 1), F32),
                        pltpu.VMEM((rows, KV_RANK), F32)])
    return pl.pallas_call(
        functools.partial(_dattn_kernel, layer=layer, pages_per_step=pages_per_step,
                          new_tokens=new_tokens),
        grid_spec=grid_spec,
        out_shape=jax.ShapeDtypeStruct((nbatch, rows, KV_RANK), BF16),
        compiler_params=_params(("arbitrary", "arbitrary")), name="dattn",
    )(page_table.reshape(-1), q, knew, cache_kv, cache_pe)


def _merge_kernel(x_ref, o_ref, cm_ref, wuv_ref, ag_ref, wout_ref, gmem_ref, wmq_ref, *rest, fuse_mem):
    if fuse_mem:
        mk_ref, mv_ref, x1_ref, om_ref, attn_sc = rest
    else:
        x1_ref, om_ref, attn_sc = rest
    for hh in range(HEADS):
        attn_sc[:, hh * V_HEAD:(hh + 1) * V_HEAD] = _dot(o_ref[hh], wuv_ref[hh])
    attn_n = _rms(attn_sc[...], ag_ref[...]).astype(BF16)
    mix = _dot(attn_n, wout_ref[0:AW, :]) + _dot(cm_ref[...], wout_ref[AW:AW + CW, :])
    x1 = x_ref[...] + mix
    x1_ref[...] = x1
    qm = _dot(_rms(x1, gmem_ref[...]).astype(BF16), wmq_ref[...]) * MEM_SCALE
    if not fuse_mem:
        om_ref[...] = qm.astype(BF16)
        return
    for hh in range(MEM_HEADS):
        cs = slice(hh * MEM_HD, (hh + 1) * MEM_HD)
        s = _dot_nt(qm[:, cs].astype(BF16), mk_ref[0, :, cs])
        p = jnp.exp(s - jnp.max(s, axis=-1, keepdims=True))
        oh = _dot(p.astype(BF16), mv_ref[0, :, cs])
        om_ref[:, cs] = (oh * (1.0 / jnp.sum(p, axis=-1, keepdims=True))).astype(BF16)


def _merge(x, o_lat, cm, w, mem_kv, *, tm, tiles_per_batch):
    rows, d = x.shape
    fuse_mem = mem_kv is not None
    row_spec = lambda width: pl.BlockSpec((tm, width), lambda i: (i, 0))
    in_specs = [row_spec(d), pl.BlockSpec((HEADS, tm, KV_RANK), lambda i: (0, i, 0)), row_spec(CW),
                _const_spec((HEADS, KV_RANK, V_HEAD)), _const_spec((1, AW)),
                _const_spec((AW + CW, d)), _const_spec((1, d)), _const_spec((d, MEM_INNER))]
    args = [x, o_lat, cm, w["wuv"], w["ag"], w["wout"], w["gmem"], w["wmq"]]
    if fuse_mem:
        mem_len = mem_kv[0].shape[1]
        mem_spec = pl.BlockSpec((1, mem_len, MEM_INNER), lambda i: (i // tiles_per_batch, 0, 0))
        in_specs += [mem_spec, mem_spec]
        args += list(mem_kv)
    return pl.pallas_call(
        functools.partial(_merge_kernel, fuse_mem=fuse_mem), grid=(rows // tm,),
        in_specs=in_specs, out_specs=[row_spec(d), row_spec(MEM_INNER)],
        out_shape=[jax.ShapeDtypeStruct((rows, d), F32), jax.ShapeDtypeStruct((rows, MEM_INNER), BF16)],
        scratch_shapes=[pltpu.VMEM((tm, AW), F32)],
        compiler_params=_params(("arbitrary",)), name="merge_mem" if fuse_mem else "merge",
    )(*args)


def _smem_kernel(q_ref, mk_ref, mv_ref, o_ref):
    def body(bi, carry):
        q = q_ref[bi]
        for hh in range(MEM_HEADS):
            cs = slice(hh * MEM_HD, (hh + 1) * MEM_HD)
            s = _dot_nt(q[:, cs], mk_ref[bi, :, cs].astype(BF16))
            p = jnp.exp(s - jnp.max(s, axis=-1, keepdims=True))
            oh = _dot(p.astype(BF16), mv_ref[bi, :, cs].astype(BF16))
            o_ref[bi, :, cs] = (oh * (1.0 / jnp.sum(p, axis=-1, keepdims=True))).astype(BF16)
        return carry
    lax.fori_loop(0, q_ref.shape[0], body, 0)


def _smem(qpad, mk, mv, *, bb):
    nbatch, rows, _ = qpad.shape
    mem_len = mk.shape[1]
    q_spec = pl.BlockSpec((bb, rows, MEM_INNER), lambda i: (i, 0, 0))
    m_spec = pl.BlockSpec((bb, mem_len, MEM_INNER), lambda i: (i, 0, 0))
    return pl.pallas_call(
        _smem_kernel, grid=(nbatch // bb,), in_specs=[q_spec, m_spec, m_spec], out_specs=q_spec,
        out_shape=jax.ShapeDtypeStruct(qpad.shape, BF16),
        compiler_params=_params(("arbitrary",)), name="smem",
    )(qpad, mk, mv)


def _ffn_kernel(x1_ref, om_ref, wmo_ref, gffn_ref, wup_ref, wdn_ref, gfin_ref, y_ref, hf_sc, acc_sc):
    f = pl.program_id(1)

    @pl.when(f == 0)
    def _():
        x2 = x1_ref[...] + _dot(om_ref[...], wmo_ref[...])
        acc_sc[...] = x2
        hf_sc[...] = _rms(x2, gffn_ref[...]).astype(BF16)

    a = jnp.maximum(_dot(hf_sc[...], wup_ref[...]), 0.0)
    acc_sc[...] += _dot((a * a).astype(BF16), wdn_ref[...])

    @pl.when(f == pl.num_programs(1) - 1)
    def _():
        y_ref[...] = _rms(acc_sc[...], gfin_ref[...])


def _ffn(x1, om, w, *, tm, tf):
    rows, d = x1.shape
    d_ff = w["wup"].shape[1]
    row_spec = lambda width: pl.BlockSpec((tm, width), lambda i, f: (i, 0))
    return pl.pallas_call(
        _ffn_kernel, grid=(rows // tm, d_ff // tf),
        in_specs=[row_spec(d), row_spec(MEM_INNER), _const_spec((MEM_INNER, d)), _const_spec((1, d)),
                  pl.BlockSpec((d, tf), lambda i, f: (0, f)), pl.BlockSpec((tf, d), lambda i, f: (f, 0)),
                  _const_spec((1, d))],
        out_specs=row_spec(d), out_shape=jax.ShapeDtypeStruct((rows, d), F32),
        scratch_shapes=[pltpu.VMEM((tm, d), BF16), pltpu.VMEM((tm, d), F32)],
        compiler_params=_params(("arbitrary", "arbitrary")), name="ffn",
    )(x1, om, w["wmo"], w["gffn"], w["wup"], w["wdn"], w["gfin"])


def _rope_tables(pos):
    half = QK_ROPE // 2
    inv = ROPE_THETA ** (-jnp.arange(half, dtype=F32) / half)
    ang = pos.astype(F32)[:, None] * inv[None, :]
    cos, sin, z = jnp.cos(ang), jnp.sin(ang), jnp.zeros_like(ang)
    return (jnp.concatenate([cos, cos, z, z], axis=1),
            jnp.concatenate([-sin, z, z, z], axis=1),
            jnp.concatenate([z, sin, z, z], axis=1))


def _layer_weights(l, norm_mix_g, w_in, q_norm_g, w_q_up, kv_norm_g, w_uk, w_uv, v_norm_g, attn_out_g,
                   cmlp_out_g, w_out, norm_mem_g, w_mem_q, w_mem_o, norm_ffn_g, w_ffn_up, w_ffn_down,
                   final_norm_g):
    d = w_in.shape[1]
    o_pe_end = Q_RANK + KV_RANK + QK_ROPE
    win = jnp.concatenate([w_in[l, :, :o_pe_end], jnp.zeros((d, LANES - QK_ROPE), F32),
                           w_in[l, :, o_pe_end:]], axis=1).astype(BF16)
    wq = w_q_up[l].reshape(Q_RANK, HEADS, QK_NOPE + QK_ROPE)
    wq = jnp.concatenate([wq, jnp.zeros((Q_RANK, HEADS, 2 * LANES - QK_NOPE - QK_ROPE), F32)], axis=2)
    row = lambda g: g.reshape(1, -1)
    return dict(
        gmix=row(norm_mix_g[l]), win=win, qg=row(q_norm_g[l]),
        wq=wq.reshape(Q_RANK, HEADS * 2 * LANES).astype(BF16), kvg=row(kv_norm_g[l]),
        wuk=jnp.transpose(w_uk[l], (1, 2, 0)).astype(BF16), vg=row(v_norm_g[l]),
        cmg=row(cmlp_out_g[l]), wuv=jnp.transpose(w_uv[l], (1, 0, 2)).astype(BF16),
        ag=row(attn_out_g[l]), wout=w_out[l].astype(BF16), gmem=row(norm_mem_g[l]),
        wmq=w_mem_q[l].astype(BF16), wmo=w_mem_o[l].astype(BF16), gffn=row(norm_ffn_g[l]),
        wup=w_ffn_up[l].astype(BF16), wdn=w_ffn_down[l].astype(BF16), gfin=row(final_norm_g))


def kernel(x_prompt, x_sample, cache_kv_latent, cache_k_rope, cache_mem_k, cache_mem_v, page_table,
           mem_prompt, norm_mix_g, w_in, q_norm_g, w_q_up, kv_norm_g, w_uk, w_uv, v_norm_g, w_spatial,
           b_spatial, attn_out_g, cmlp_out_g, w_out, norm_mem_g, mem_in_g, w_mem_q, w_mem_k, w_mem_v,
           w_mem_o, norm_ffn_g, w_ffn_up, w_ffn_down, final_norm_g):
    batch, seq, d = x_prompt.shape
    dbatch, t_new, _ = x_sample.shape
    depth = w_in.shape[0]
    assert depth == 1 and seq % CHUNK == 0 and CHUNK % t_new == 0
    l = 0
    mem_len = mem_prompt.shape[1]
    tm = 256
    srows = dbatch * t_new

    w = _layer_weights(l, norm_mix_g, w_in, q_norm_g, w_q_up, kv_norm_g, w_uk, w_uv, v_norm_g,
                       attn_out_g, cmlp_out_g, w_out, norm_mem_g, w_mem_q, w_mem_o, norm_ffn_g,
                       w_ffn_up, w_ffn_down, final_norm_g)

    xp = x_prompt.reshape(batch * seq, d)
    tabs_p = _rope_tables(jnp.arange(seq))
    q_p, kmat_p, kvc_p, kpe_p, cm_p = _proj(
        xp, tabs_p, seq // tm, w_spatial[l], b_spatial[l].T, w, tm=tm, emit_v=False)
    mk_p, mv_p = _memkv(mem_prompt.reshape(batch * mem_len, d), mem_in_g[l].reshape(1, -1),
                        w_mem_k[l].astype(BF16), w_mem_v[l].astype(BF16), tm=tm)
    o_p = _pattn(q_p, kmat_p, batch=batch, seq=seq, tq=256)
    mem_kv_p = (mk_p.astype(BF16).reshape(batch, mem_len, MEM_INNER),
                mv_p.astype(BF16).reshape(batch, mem_len, MEM_INNER))
    x1_p, om_p = _merge(xp, o_p, cm_p, w, mem_kv_p, tm=tm, tiles_per_batch=seq // tm)
    y_p = _ffn(x1_p, om_p, w, tm=512, tf=1024)

    xs = x_sample.reshape(srows, d)
    reps = CHUNK // t_new
    tabs_s = _rope_tables(PAST_LEN + jnp.arange(tm) % t_new)
    eye = jnp.eye(reps, dtype=F32)
    wmix_s = jnp.einsum("ab,gts->gatbs", eye, w_spatial[l][:, :t_new, :t_new]).reshape(GROUPS, CHUNK, CHUNK)
    bmix_s = jnp.tile(b_spatial[l][:, :t_new].T, (reps, 1))
    q_s, kmat_s, kvc_s, kpe_s, cm_s, v_s = _proj(xs, tabs_s, 1, wmix_s, bmix_s, w, tm=tm, emit_v=True)

    q_sb = jnp.transpose(q_s.reshape(HEADS, dbatch, t_new, KW), (1, 0, 2, 3)).reshape(dbatch, HEADS * t_new, KW)
    knew = jnp.pad(kmat_s.reshape(dbatch, t_new, KW), ((0, 0), (0, LANES - t_new), (0, 0)))
    o_sb = _dattn(page_table, q_sb, knew, cache_kv_latent, cache_k_rope,
                  layer=l, pages_per_step=32, new_tokens=t_new)
    o_s = jnp.transpose(o_sb.reshape(dbatch, HEADS, t_new, KV_RANK), (1, 0, 2, 3)).reshape(HEADS, srows, KV_RANK)
    x1_s, qm_s = _merge(xs, o_s, cm_s, w, None, tm=tm, tiles_per_batch=1)
    qrows = 16
    qm_pad = jnp.pad(qm_s.reshape(dbatch, t_new, MEM_INNER), ((0, 0), (0, qrows - t_new), (0, 0)))
    om_s = _smem(qm_pad, cache_mem_k[l].reshape(dbatch, mem_len, MEM_INNER),
                 cache_mem_v[l].reshape(dbatch, mem_len, MEM_INNER), bb=8)
    om_s = om_s[:, :t_new].reshape(srows, MEM_INNER)
    y_s = _ffn(x1_s, om_s, w, tm=512, tf=1024)

    mem_shape = (depth, batch, mem_len, MEM_HEADS, MEM_HD)
    return (y_p.reshape(batch, seq, d), y_s.reshape(dbatch, t_new, d),
            kvc_p.reshape(depth, batch, seq, KV_RANK), kpe_p.reshape(depth, batch, seq, QK_ROPE),
            mk_p.reshape(mem_shape), mv_p.reshape(mem_shape),
            kvc_s.reshape(depth, dbatch, t_new, KV_RANK), kpe_s.reshape(depth, dbatch, t_new, QK_ROPE),
            v_s.reshape(depth, dbatch, t_new, GROUPS, GW))
```

```python
import functools

import jax
import jax.numpy as jnp
from jax import lax
from jax.experimental import pallas as pl
from jax.experimental.pallas import tpu as pltpu

F32 = jnp.float32
BF16 = jnp.bfloat16

EPS = 1e-6
ROPE_THETA = 10000.0
PAST_LEN = 16384

LANES = 128
V7X_VMEM_BYTES = 64 * 2**20
VMEM_LIMIT = V7X_VMEM_BYTES - 8 * 2**20

Q_RANK = 512
KV_RANK = 256
QK_NOPE = 128
QK_ROPE = 64
HEADS = 8
V_HEAD = 128
GROUPS = 8
GW = 128
CHUNK = 128
CW = GROUPS * GW
AW = HEADS * V_HEAD
KW = KV_RANK + LANES
MEM_HEADS = 4
MEM_HD = 128
MEM_INNER = MEM_HEADS * MEM_HD
MLA_SCALE = (QK_NOPE + QK_ROPE) ** -0.5
MEM_SCALE = MEM_HD ** -0.5

O_Q = 0
O_KV = O_Q + Q_RANK
O_PE = O_KV + KV_RANK
O_U = O_PE + LANES
O_V = O_U + CW
D_IN_PAD = O_V + CW

NT_DIMS = (((1,), (1,)), ((), ()))


def _const_spec(shape):
    return pl.BlockSpec(shape, lambda *_: (0,) * len(shape), pipeline_mode=pl.Buffered(1))


def _params(semantics):
    return pltpu.CompilerParams(dimension_semantics=semantics, vmem_limit_bytes=VMEM_LIMIT)


def _rms(x, g):
    return x * lax.rsqrt(jnp.mean(x * x, axis=-1, keepdims=True) + EPS) * g


def _gelu(x):
    return 0.5 * x * (1.0 + jnp.tanh(0.7978845608028654 * (x + 0.044715 * (x * x * x))))


def _dot(a, b):
    return jnp.dot(a, b, preferred_element_type=F32)


def _dot_nt(a, b):
    return lax.dot_general(a, b, NT_DIMS, preferred_element_type=F32)


def _proj_kernel(x_ref, gmix_ref, win_ref, qg_ref, wq_ref, kvg_ref, wuk_ref, vg_ref,
                 cos_ref, s1_ref, s2_ref, wmix_ref, bmix_ref, cmg_ref,
                 q_ref, kmat_ref, kvc_ref, kpe_ref, cm_ref, extra_ref, cm_sc, *, prompt):
    tm = x_ref.shape[0]
    h = _rms(x_ref[...], gmix_ref[...]).astype(BF16)

    def seg(lo, hi):
        return _dot(h, win_ref[:, lo:hi])

    cosv, s1, s2 = cos_ref[...], s1_ref[...], s2_ref[...]

    def rope(p):
        return (p * cosv + pltpu.roll(p, LANES - QK_ROPE // 2, 1) * s1
                + pltpu.roll(p, QK_ROPE // 2, 1) * s2)

    qc = _rms(seg(O_Q, O_KV), qg_ref[...]).astype(BF16)
    q = _dot(qc, wq_ref[...])
    for hh in range(HEADS):
        base = hh * 2 * LANES
        qa = _dot(q[:, base:base + QK_NOPE].astype(BF16), wuk_ref[hh]) * MLA_SCALE
        qp = rope(q[:, base + LANES:base + 2 * LANES]) * MLA_SCALE
        if prompt:
            cols = slice(hh * tm, (hh + 1) * tm)
            q_ref[0:KV_RANK, cols] = qa.T.astype(BF16)
            q_ref[KV_RANK:KW, cols] = qp.T.astype(BF16)
        else:
            q_ref[hh, :, 0:KV_RANK] = qa.astype(BF16)
            q_ref[hh, :, KV_RANK:KW] = qp.astype(BF16)

    kv = _rms(seg(O_KV, O_PE), kvg_ref[...])
    kvc_ref[...] = kv
    kmat_ref[:, 0:KV_RANK] = kv.astype(BF16)
    if prompt:
        extra_ref[...] = kv.T.astype(BF16)
    kp = rope(seg(O_PE, O_U))
    kpe_ref[...] = kp[:, 0:QK_ROPE]
    kmat_ref[:, KV_RANK:KW] = kp.astype(BF16)

    u = _gelu(seg(O_U, O_V))
    v = _rms(_gelu(seg(O_V, D_IN_PAD)), vg_ref[...])
    if not prompt:
        extra_ref[...] = v
    vb = v.astype(BF16)
    row = lax.broadcasted_iota(jnp.int32, (CHUNK, CHUNK), 0)
    col = lax.broadcasted_iota(jnp.int32, (CHUNK, CHUNK), 1)
    for g in range(GROUPS):
        wg = jnp.where(row >= col, wmix_ref[g], 0.0).astype(BF16)
        bg = bmix_ref[:, g:g + 1]
        for c in range(tm // CHUNK):
            rs = slice(c * CHUNK, (c + 1) * CHUNK)
            cs = slice(g * GW, (g + 1) * GW)
            cm_sc[rs, cs] = u[rs, cs] * (_dot(wg, vb[rs, cs]) + bg)
    cm_ref[...] = _rms(cm_sc[...], cmg_ref[...]).astype(BF16)


def _proj(x, tabs, tab_blocks, wmix, bmix, w, *, tm, prompt):
    rows, d = x.shape
    n = rows // tm
    cos_t, s1_t, s2_t = tabs
    tab_spec = pl.BlockSpec((tm, LANES), lambda i: (i % tab_blocks, 0))
    row_spec = lambda width: pl.BlockSpec((tm, width), lambda i: (i, 0))
    in_specs = [
        row_spec(d), _const_spec((1, d)), _const_spec((d, D_IN_PAD)), _const_spec((1, Q_RANK)),
        _const_spec((Q_RANK, HEADS * 2 * LANES)), _const_spec((1, KV_RANK)),
        _const_spec((HEADS, QK_NOPE, KV_RANK)), _const_spec((1, CW)),
        tab_spec, tab_spec, tab_spec,
        _const_spec((GROUPS, CHUNK, CHUNK)), _const_spec((CHUNK, GROUPS)), _const_spec((1, CW)),
    ]
    if prompt:
        q_shape, q_spec = (KW, HEADS * rows), pl.BlockSpec((KW, HEADS * tm), lambda i: (0, i))
        extra_shape = jax.ShapeDtypeStruct((KV_RANK, rows), BF16)
        extra_spec = pl.BlockSpec((KV_RANK, tm), lambda i: (0, i))
    else:
        q_shape, q_spec = (HEADS, rows, KW), pl.BlockSpec((HEADS, tm, KW), lambda i: (0, i, 0))
        extra_shape, extra_spec = jax.ShapeDtypeStruct((rows, CW), F32), row_spec(CW)
    out_shape = [
        jax.ShapeDtypeStruct(q_shape, BF16),
        jax.ShapeDtypeStruct((rows, KW), BF16),
        jax.ShapeDtypeStruct((rows, KV_RANK), F32),
        jax.ShapeDtypeStruct((rows, QK_ROPE), F32),
        jax.ShapeDtypeStruct((rows, CW), BF16),
        extra_shape,
    ]
    out_specs = [q_spec, row_spec(KW), row_spec(KV_RANK), row_spec(QK_ROPE), row_spec(CW), extra_spec]
    return pl.pallas_call(
        functools.partial(_proj_kernel, prompt=prompt),
        grid=(n,), in_specs=in_specs, out_specs=out_specs, out_shape=out_shape,
        scratch_shapes=[pltpu.VMEM((tm, CW), F32)],
        compiler_params=_params(("arbitrary",)),
        name="proj_prompt" if prompt else "proj_sample",
    )(x, w["gmix"], w["win"], w["qg"], w["wq"], w["kvg"], w["wuk"], w["vg"],
      cos_t, s1_t, s2_t, wmix, bmix, w["cmg"])


def _memkv_kernel(m_ref, g_ref, wk_ref, wv_ref, k_ref, v_ref):
    m = _rms(m_ref[...], g_ref[...]).astype(BF16)
    k_ref[...] = _dot(m, wk_ref[...])
    v_ref[...] = _dot(m, wv_ref[...])


def _memkv(mem, g, wk, wv, *, tm):
    rows, d = mem.shape
    row_spec = lambda width: pl.BlockSpec((tm, width), lambda i: (i, 0))
    return pl.pallas_call(
        _memkv_kernel, grid=(rows // tm,),
        in_specs=[row_spec(d), _const_spec((1, d)), _const_spec((d, MEM_INNER)),
                  _const_spec((d, MEM_INNER))],
        out_specs=[row_spec(MEM_INNER), row_spec(MEM_INNER)],
        out_shape=[jax.ShapeDtypeStruct((rows, MEM_INNER), F32)] * 2,
        compiler_params=_params(("arbitrary",)), name="memkv",
    )(mem, g, wk, wv)


def _online_step(s, v, m, l, acc):
    m_new = jnp.maximum(m, jnp.max(s, axis=-1, keepdims=True))
    p = jnp.exp(s - m_new)
    alpha = jnp.exp(m - m_new)
    return (m_new, alpha * l + jnp.sum(p, axis=-1, keepdims=True),
            alpha * acc + _dot(p.astype(v.dtype), v))


def _online_init(m_sc, l_sc, acc_sc):
    m_sc[...] = jnp.full(m_sc.shape, -jnp.inf, F32)
    l_sc[...] = jnp.zeros(l_sc.shape, F32)
    acc_sc[...] = jnp.zeros(acc_sc.shape, F32)


def _pattn_kernel(qt_ref, k_ref, kvt_ref, o_ref, m_sc, l_sc, acc_sc):
    i = pl.program_id(1)
    heads, tq, _ = o_ref.shape
    cols = heads * tq
    _online_init(m_sc, l_sc, acc_sc)

    def block(j, masked):
        keys = pl.ds(pl.multiple_of(j * tq, tq), tq)
        st = _dot(k_ref[keys, :], qt_ref[...])
        if masked:
            key = lax.broadcasted_iota(jnp.int32, (tq, cols), 0)
            query = lax.broadcasted_iota(jnp.int32, (tq, cols), 1) % tq
            st = jnp.where(key <= query, st, -jnp.inf)
        m_prev = m_sc[...]
        m_new = jnp.maximum(m_prev, jnp.max(st, axis=0, keepdims=True))
        p = jnp.exp(st - m_new)
        alpha = jnp.exp(m_prev - m_new)
        l_sc[...] = alpha * l_sc[...] + jnp.sum(p, axis=0, keepdims=True)
        acc_sc[...] = alpha * acc_sc[...] + _dot(kvt_ref[:, keys], p.astype(BF16))
        m_sc[...] = m_new

    def full_block(j, carry):
        block(j, False)
        return carry

    lax.fori_loop(0, i, full_block, 0)
    block(i, True)

    ot = acc_sc[...] * (1.0 / l_sc[...])
    for hh in range(heads):
        o_ref[hh] = ot[:, hh * tq:(hh + 1) * tq].T.astype(BF16)


def _pattn(qt, kmat, kvt, *, batch, seq, tq):
    heads = qt.shape[1] // (batch * seq)
    nq = seq // tq
    return pl.pallas_call(
        _pattn_kernel, grid=(batch, nq),
        in_specs=[pl.BlockSpec((KW, heads * tq), lambda b, i: (0, b * nq + i)),
                  pl.BlockSpec((seq, KW), lambda b, i: (b, 0)),
                  pl.BlockSpec((KV_RANK, seq), lambda b, i: (0, b))],
        out_specs=pl.BlockSpec((heads, tq, KV_RANK), lambda b, i: (0, b * nq + i, 0)),
        out_shape=jax.ShapeDtypeStruct((heads, batch * seq, KV_RANK), BF16),
        scratch_shapes=[pltpu.VMEM((1, heads * tq), F32), pltpu.VMEM((1, heads * tq), F32),
                        pltpu.VMEM((KV_RANK, heads * tq), F32)],
        compiler_params=_params(("arbitrary", "arbitrary")), name="pattn",
    )(qt, kmat, kvt)


def _dattn_kernel(pt_ref, q_ref, kn_ref, ckv_hbm, cpe_hbm, o_ref,
                  kvbuf0, kvbuf1, pebuf0, pebuf1, sems, m_sc, l_sc, acc_sc, *, layer, pages_per_chunk,
                  key_splits, new_tokens):
    b = pl.program_id(0)
    c = pl.program_id(1)
    nb = pl.num_programs(0)
    nc = pl.num_programs(1)
    kvbufs, pebufs = (kvbuf0, kvbuf1), (pebuf0, pebuf1)
    chunk_keys = kvbuf0.shape[0]
    page = chunk_keys // pages_per_chunk
    sub = chunk_keys // key_splits
    pages_per_step = 2 * pages_per_chunk
    first_page = (b * nc + c) * pages_per_step

    def start_chunk(first, buf):
        for j in range(pages_per_chunk):
            page_id = pt_ref[first + j]
            keys = slice(j * page, (j + 1) * page)
            pltpu.make_async_copy(ckv_hbm.at[layer, page_id], kvbufs[buf].at[keys], sems.at[0, buf]).start()
            pltpu.make_async_copy(cpe_hbm.at[layer, page_id], pebufs[buf].at[:, keys], sems.at[1, buf]).start()

    def wait_chunk(buf):
        pltpu.make_async_copy(kvbufs[buf], kvbufs[buf], sems.at[0, buf]).wait()
        pltpu.make_async_copy(pebufs[buf], pebufs[buf], sems.at[1, buf]).wait()

    q = q_ref[0]
    qa = q[:, 0:KV_RANK].astype(F32)
    qp = q[:, KV_RANK:KV_RANK + QK_ROPE].astype(F32)

    def scores(buf, k):
        keys = slice(k * sub, (k + 1) * sub)
        return _dot_nt(qa, kvbufs[buf][keys, :]) + _dot(qp, pebufs[buf][:, keys])

    def attend_chunk(buf, state):
        s_next = scores(buf, 0)
        for k in range(key_splits):
            s = s_next
            if k + 1 < key_splits:
                s_next = scores(buf, k + 1)
            state = _online_step(s, kvbufs[buf][k * sub:(k + 1) * sub, :], *state)
        return state

    last_step = b * nc + c + 1 == nb * nc

    @pl.when((b == 0) & (c == 0))
    def _():
        start_chunk(first_page, 0)

    @pl.when(c == 0)
    def _():
        _online_init(m_sc, l_sc, acc_sc)

    wait_chunk(0)
    start_chunk(first_page + pages_per_chunk, 1)
    state = attend_chunk(0, (m_sc[...], l_sc[...], acc_sc[...]))
    wait_chunk(1)
    start_chunk(jnp.where(last_step, 0, first_page + pages_per_step), 0)
    state = attend_chunk(1, state)
    m_sc[...], l_sc[...], acc_sc[...] = state

    @pl.when(last_step)
    def _():
        wait_chunk(0)

    @pl.when(c == nc - 1)
    def _():
        kn = kn_ref[0]
        sn = _dot_nt(q, kn)
        t = lax.broadcasted_iota(jnp.int32, sn.shape, 0) % new_tokens
        cidx = lax.broadcasted_iota(jnp.int32, sn.shape, 1)
        _, l, acc = _online_step(jnp.where(cidx <= t, sn, -jnp.inf), kn[:, 0:KV_RANK], *state)
        o_ref[0] = (acc * (1.0 / l)).astype(BF16)


def _dattn(page_table, q, knew, cache_kv, cache_pe, *, layer, pages_per_chunk, key_splits, new_tokens):
    nbatch, rows, _ = q.shape
    n_pages = page_table.shape[1]
    page = cache_kv.shape[2]
    nc = n_pages // (2 * pages_per_chunk)
    keys = pages_per_chunk * page
    grid_spec = pltpu.PrefetchScalarGridSpec(
        num_scalar_prefetch=1, grid=(nbatch, nc),
        in_specs=[pl.BlockSpec((1, rows, KW), lambda b, c, pt: (b, 0, 0)),
                  pl.BlockSpec((1, LANES, KW), lambda b, c, pt: (b, 0, 0)),
                  pl.BlockSpec(memory_space=pl.ANY), pl.BlockSpec(memory_space=pl.ANY)],
        out_specs=pl.BlockSpec((1, rows, KV_RANK), lambda b, c, pt: (b, 0, 0)),
        scratch_shapes=[pltpu.VMEM((keys, KV_RANK), F32), pltpu.VMEM((keys, KV_RANK), F32),
                        pltpu.VMEM((QK_ROPE, keys), F32), pltpu.VMEM((QK_ROPE, keys), F32),
                        pltpu.SemaphoreType.DMA((2, 2)),
                        pltpu.VMEM((rows, 1), F32), pltpu.VMEM((rows, 1), F32),
                        pltpu.VMEM((rows, KV_RANK), F32)])
    return pl.pallas_call(
        functools.partial(_dattn_kernel, layer=layer, pages_per_chunk=pages_per_chunk,
                          key_splits=key_splits, new_tokens=new_tokens),
        grid_spec=grid_spec,
        out_shape=jax.ShapeDtypeStruct((nbatch, rows, KV_RANK), BF16),
        compiler_params=_params(("arbitrary", "arbitrary")), name="dattn",
    )(page_table.reshape(-1), q, knew, cache_kv, cache_pe)


def _merge_kernel(x_ref, o_ref, cm_ref, wuv_ref, ag_ref, wout_ref, gmem_ref, wmq_ref, *rest, fuse_mem):
    if fuse_mem:
        mk_ref, mv_ref, x1_ref, om_ref, attn_sc = rest
    else:
        x1_ref, om_ref, attn_sc = rest
    for hh in range(HEADS):
        attn_sc[:, hh * V_HEAD:(hh + 1) * V_HEAD] = _dot(o_ref[hh], wuv_ref[hh])
    attn_n = _rms(attn_sc[...], ag_ref[...]).astype(BF16)
    mix = _dot(attn_n, wout_ref[0:AW, :]) + _dot(cm_ref[...], wout_ref[AW:AW + CW, :])
    x1 = x_ref[...] + mix
    x1_ref[...] = x1
    qm = _dot(_rms(x1, gmem_ref[...]).astype(BF16), wmq_ref[...]) * MEM_SCALE
    if not fuse_mem:
        om_ref[...] = qm.astype(BF16)
        return
    for hh in range(MEM_HEADS):
        cs = slice(hh * MEM_HD, (hh + 1) * MEM_HD)
        s = _dot_nt(qm[:, cs].astype(BF16), mk_ref[0, :, cs])
        p = jnp.exp(s - jnp.max(s, axis=-1, keepdims=True))
        oh = _dot(p.astype(BF16), mv_ref[0, :, cs])
        om_ref[:, cs] = (oh * (1.0 / jnp.sum(p, axis=-1, keepdims=True))).astype(BF16)


def _merge(x, o_lat, cm, w, mem_kv, *, tm, tiles_per_batch):
    rows, d = x.shape
    fuse_mem = mem_kv is not None
    row_spec = lambda width: pl.BlockSpec((tm, width), lambda i: (i, 0))
    in_specs = [row_spec(d), pl.BlockSpec((HEADS, tm, KV_RANK), lambda i: (0, i, 0)), row_spec(CW),
                _const_spec((HEADS, KV_RANK, V_HEAD)), _const_spec((1, AW)),
                _const_spec((AW + CW, d)), _const_spec((1, d)), _const_spec((d, MEM_INNER))]
    args = [x, o_lat, cm, w["wuv"], w["ag"], w["wout"], w["gmem"], w["wmq"]]
    if fuse_mem:
        mem_len = mem_kv[0].shape[1]
        mem_spec = pl.BlockSpec((1, mem_len, MEM_INNER), lambda i: (i // tiles_per_batch, 0, 0))
        in_specs += [mem_spec, mem_spec]
        args += list(mem_kv)
    return pl.pallas_call(
        functools.partial(_merge_kernel, fuse_mem=fuse_mem), grid=(rows // tm,),
        in_specs=in_specs, out_specs=[row_spec(d), row_spec(MEM_INNER)],
        out_shape=[jax.ShapeDtypeStruct((rows, d), F32), jax.ShapeDtypeStruct((rows, MEM_INNER), BF16)],
        scratch_shapes=[pltpu.VMEM((tm, AW), F32)],
        compiler_params=_params(("arbitrary",)), name="merge_mem" if fuse_mem else "merge",
    )(*args)


def _smem_kernel(q_ref, mk_ref, mv_ref, o_ref):
    rows, mrows = q_ref.shape[1], mk_ref.shape[1]
    qhead = lax.broadcasted_iota(jnp.int32, (rows, mrows), 0) % MEM_HEADS
    mhead = lax.broadcasted_iota(jnp.int32, (rows, mrows), 1) % MEM_HEADS
    same_head = qhead == mhead

    def body(bi, carry):
        s = _dot_nt(q_ref[bi], mk_ref[bi].astype(BF16))
        s = jnp.where(same_head, s, -jnp.inf)
        p = jnp.exp(s - jnp.max(s, axis=-1, keepdims=True))
        o = _dot(p.astype(BF16), mv_ref[bi].astype(BF16))
        o_ref[bi] = (o * (1.0 / jnp.sum(p, axis=-1, keepdims=True))).astype(BF16)
        return carry
    lax.fori_loop(0, q_ref.shape[0], body, 0)


def _smem(q, mk, mv, *, layer, bb):
    nbatch, rows, _ = q.shape
    mrows = mk.shape[1]
    nblk = nbatch // bb
    q_spec = pl.BlockSpec((bb, rows, MEM_HD), lambda i: (i, 0, 0))
    m_spec = pl.BlockSpec((bb, mrows, MEM_HD), lambda i: (layer * nblk + i, 0, 0))
    return pl.pallas_call(
        _smem_kernel, grid=(nblk,), in_specs=[q_spec, m_spec, m_spec], out_specs=q_spec,
        out_shape=jax.ShapeDtypeStruct(q.shape, BF16),
        compiler_params=_params(("arbitrary",)), name="smem",
    )(q, mk, mv)


def _ffn_kernel(x1_ref, om_ref, wmo_ref, gffn_ref, wup_ref, wdn_ref, gfin_ref, y_ref, hf_sc, acc_sc):
    f = pl.program_id(1)

    @pl.when(f == 0)
    def _():
        x2 = x1_ref[...] + _dot(om_ref[...], wmo_ref[...])
        acc_sc[...] = x2
        hf_sc[...] = _rms(x2, gffn_ref[...]).astype(BF16)

    a = jnp.maximum(_dot(hf_sc[...], wup_ref[...]), 0.0)
    acc_sc[...] += _dot((a * a).astype(BF16), wdn_ref[...])

    @pl.when(f == pl.num_programs(1) - 1)
    def _():
        y_ref[...] = _rms(acc_sc[...], gfin_ref[...])


def _ffn(x1, om, w, *, tm, tf):
    rows, d = x1.shape
    d_ff = w["wup"].shape[1]
    row_spec = lambda width: pl.BlockSpec((tm, width), lambda i, f: (i, 0))
    return pl.pallas_call(
        _ffn_kernel, grid=(rows // tm, d_ff // tf),
        in_specs=[row_spec(d), row_spec(MEM_INNER), _const_spec((MEM_INNER, d)), _const_spec((1, d)),
                  pl.BlockSpec((d, tf), lambda i, f: (0, f)), pl.BlockSpec((tf, d), lambda i, f: (f, 0)),
                  _const_spec((1, d))],
        out_specs=row_spec(d), out_shape=jax.ShapeDtypeStruct((rows, d), F32),
        scratch_shapes=[pltpu.VMEM((tm, d), BF16), pltpu.VMEM((tm, d), F32)],
        compiler_params=_params(("arbitrary", "arbitrary")), name="ffn",
    )(x1, om, w["wmo"], w["gffn"], w["wup"], w["wdn"], w["gfin"])


def _rope_tables(pos):
    half = QK_ROPE // 2
    inv = ROPE_THETA ** (-jnp.arange(half, dtype=F32) / half)
    ang = pos.astype(F32)[:, None] * inv[None, :]
    cos, sin, z = jnp.cos(ang), jnp.sin(ang), jnp.zeros_like(ang)
    return (jnp.concatenate([cos, cos, z, z], axis=1),
            jnp.concatenate([-sin, z, z, z], axis=1),
            jnp.concatenate([z, sin, z, z], axis=1))


def _layer_weights(l, norm_mix_g, w_in, q_norm_g, w_q_up, kv_norm_g, w_uk, w_uv, v_norm_g, attn_out_g,
                   cmlp_out_g, w_out, norm_mem_g, w_mem_q, w_mem_o, norm_ffn_g, w_ffn_up, w_ffn_down,
                   final_norm_g):
    d = w_in.shape[1]
    o_pe_end = Q_RANK + KV_RANK + QK_ROPE
    win = jnp.concatenate([w_in[l, :, :o_pe_end], jnp.zeros((d, LANES - QK_ROPE), F32),
                           w_in[l, :, o_pe_end:]], axis=1).astype(BF16)
    wq = w_q_up[l].reshape(Q_RANK, HEADS, QK_NOPE + QK_ROPE)
    wq = jnp.concatenate([wq, jnp.zeros((Q_RANK, HEADS, 2 * LANES - QK_NOPE - QK_ROPE), F32)], axis=2)
    row = lambda g: g.reshape(1, -1)
    return dict(
        gmix=row(norm_mix_g[l]), win=win, qg=row(q_norm_g[l]),
        wq=wq.reshape(Q_RANK, HEADS * 2 * LANES).astype(BF16), kvg=row(kv_norm_g[l]),
        wuk=jnp.transpose(w_uk[l], (1, 2, 0)).astype(BF16), vg=row(v_norm_g[l]),
        cmg=row(cmlp_out_g[l]), wuv=jnp.transpose(w_uv[l], (1, 0, 2)).astype(BF16),
        ag=row(attn_out_g[l]), wout=w_out[l].astype(BF16), gmem=row(norm_mem_g[l]),
        wmq=w_mem_q[l].astype(BF16), wmo=w_mem_o[l].astype(BF16), gffn=row(norm_ffn_g[l]),
        wup=w_ffn_up[l].astype(BF16), wdn=w_ffn_down[l].astype(BF16), gfin=row(final_norm_g))


def kernel(x_prompt, x_sample, cache_kv_latent, cache_k_rope, cache_mem_k, cache_mem_v, page_table,
           mem_prompt, norm_mix_g, w_in, q_norm_g, w_q_up, kv_norm_g, w_uk, w_uv, v_norm_g, w_spatial,
           b_spatial, attn_out_g, cmlp_out_g, w_out, norm_mem_g, mem_in_g, w_mem_q, w_mem_k, w_mem_v,
           w_mem_o, norm_ffn_g, w_ffn_up, w_ffn_down, final_norm_g):
    batch, seq, d = x_prompt.shape
    dbatch, t_new, _ = x_sample.shape
    depth = w_in.shape[0]
    assert depth == 1 and seq % CHUNK == 0 and CHUNK % t_new == 0
    l = 0
    mem_len = mem_prompt.shape[1]
    tm = 256
    srows = dbatch * t_new

    w = _layer_weights(l, norm_mix_g, w_in, q_norm_g, w_q_up, kv_norm_g, w_uk, w_uv, v_norm_g,
                       attn_out_g, cmlp_out_g, w_out, norm_mem_g, w_mem_q, w_mem_o, norm_ffn_g,
                       w_ffn_up, w_ffn_down, final_norm_g)

    xp = x_prompt.reshape(batch * seq, d)
    tabs_p = _rope_tables(jnp.arange(seq))
    qt_p, kmat_p, kvc_p, kpe_p, cm_p, kvt_p = _proj(
        xp, tabs_p, seq // tm, w_spatial[l], b_spatial[l].T, w, tm=tm, prompt=True)
    mk_p, mv_p = _memkv(mem_prompt.reshape(batch * mem_len, d), mem_in_g[l].reshape(1, -1),
                        w_mem_k[l].astype(BF16), w_mem_v[l].astype(BF16), tm=tm)
    o_p = _pattn(qt_p, kmat_p, kvt_p, batch=batch, seq=seq, tq=tm)
    mem_kv_p = (mk_p.astype(BF16).reshape(batch, mem_len, MEM_INNER),
                mv_p.astype(BF16).reshape(batch, mem_len, MEM_INNER))
    x1_p, om_p = _merge(xp, o_p, cm_p, w, mem_kv_p, tm=tm, tiles_per_batch=seq // tm)
    y_p = _ffn(x1_p, om_p, w, tm=512, tf=1024)

    xs = x_sample.reshape(srows, d)
    reps = CHUNK // t_new
    tabs_s = _rope_tables(PAST_LEN + jnp.arange(tm) % t_new)
    eye = jnp.eye(reps, dtype=F32)
    wmix_s = jnp.einsum("ab,gts->gatbs", eye, w_spatial[l][:, :t_new, :t_new]).reshape(GROUPS, CHUNK, CHUNK)
    bmix_s = jnp.tile(b_spatial[l][:, :t_new].T, (reps, 1))
    q_s, kmat_s, kvc_s, kpe_s, cm_s, v_s = _proj(xs, tabs_s, 1, wmix_s, bmix_s, w, tm=tm, prompt=False)

    q_sb = jnp.transpose(q_s.reshape(HEADS, dbatch, t_new, KW), (1, 0, 2, 3)).reshape(dbatch, HEADS * t_new, KW)
    knew = jnp.pad(kmat_s.reshape(dbatch, t_new, KW), ((0, 0), (0, LANES - t_new), (0, 0)))
    o_sb = _dattn(page_table, q_sb, knew, cache_kv_latent, jnp.swapaxes(cache_k_rope, 2, 3),
                  layer=l, pages_per_chunk=32, key_splits=4, new_tokens=t_new)
    o_s = jnp.transpose(o_sb.reshape(dbatch, HEADS, t_new, KV_RANK), (1, 0, 2, 3)).reshape(HEADS, srows, KV_RANK)
    x1_s, qm_s = _merge(xs, o_s, cm_s, w, None, tm=tm, tiles_per_batch=1)
    mem_rows = (depth * dbatch, mem_len * MEM_HEADS, MEM_HD)
    om_s = _smem(qm_s.reshape(dbatch, t_new * MEM_HEADS, MEM_HD), cache_mem_k.reshape(mem_rows),
                 cache_mem_v.reshape(mem_rows), layer=l, bb=8)
    y_s = _ffn(x1_s, om_s.reshape(srows, MEM_INNER), w, tm=512, tf=1024)

    mem_shape = (depth, batch, mem_len, MEM_HEADS, MEM_HD)
    return (y_p.reshape(batch, seq, d), y_s.reshape(dbatch, t_new, d),
            kvc_p.reshape(depth, batch, seq, KV_RANK), kpe_p.reshape(depth, batch, seq, QK_ROPE),
            mk_p.reshape(mem_shape), mv_p.reshape(mem_shape),
            kvc_s.reshape(depth, dbatch, t_new, KV_RANK), kpe_s.reshape(depth, dbatch, t_new, QK_ROPE),
            v_s.reshape(depth, dbatch, t_new, GROUPS, GW))
```

```python
import functools

import jax
import jax.numpy as jnp
from jax import lax
from jax.experimental import pallas as pl
from jax.experimental.pallas import tpu as pltpu

F32 = jnp.float32
BF16 = jnp.bfloat16

EPS = 1e-6
ROPE_THETA = 10000.0
PAST_LEN = 16384

LANES = 128
V7X_VMEM_BYTES = 64 * 2**20
VMEM_LIMIT = V7X_VMEM_BYTES - 8 * 2**20

Q_RANK = 512
KV_RANK = 256
QK_NOPE = 128
QK_ROPE = 64
HEADS = 8
V_HEAD = 128
GROUPS = 8
GW = 128
CHUNK = 128
CW = GROUPS * GW
AW = HEADS * V_HEAD
KW = KV_RANK + LANES
MEM_HEADS = 4
MEM_HD = 128
MEM_INNER = MEM_HEADS * MEM_HD
MLA_SCALE = (QK_NOPE + QK_ROPE) ** -0.5
MEM_SCALE = MEM_HD ** -0.5

O_Q = 0
O_KV = O_Q + Q_RANK
O_PE = O_KV + KV_RANK
O_U = O_PE + LANES
O_V = O_U + CW
D_IN_PAD = O_V + CW

NT_DIMS = (((1,), (1,)), ((), ()))


def _const_spec(shape):
    return pl.BlockSpec(shape, lambda *_: (0,) * len(shape), pipeline_mode=pl.Buffered(1))


def _params(semantics):
    return pltpu.CompilerParams(dimension_semantics=semantics, vmem_limit_bytes=VMEM_LIMIT)


def _rms(x, g):
    return x * lax.rsqrt(jnp.mean(x * x, axis=-1, keepdims=True) + EPS) * g


def _gelu(x):
    return 0.5 * x * (1.0 + jnp.tanh(0.7978845608028654 * (x + 0.044715 * (x * x * x))))


def _dot(a, b):
    return jnp.dot(a, b, preferred_element_type=F32)


def _dot_nt(a, b):
    return lax.dot_general(a, b, NT_DIMS, preferred_element_type=F32)


def _proj_kernel(x_ref, gmix_ref, win_ref, qg_ref, wq_ref, kvg_ref, wuk_ref, vg_ref,
                 cos_ref, s1_ref, s2_ref, wmix_ref, bmix_ref, cmg_ref,
                 q_ref, kmat_ref, kvc_ref, kpe_ref, cm_ref, extra_ref, cm_sc, *, prompt):
    tm = x_ref.shape[0]
    h = _rms(x_ref[...], gmix_ref[...]).astype(BF16)

    def seg(lo, hi):
        return _dot(h, win_ref[:, lo:hi])

    cosv, s1, s2 = cos_ref[...], s1_ref[...], s2_ref[...]

    def rope(p):
        return (p * cosv + pltpu.roll(p, LANES - QK_ROPE // 2, 1) * s1
                + pltpu.roll(p, QK_ROPE // 2, 1) * s2)

    qc = _rms(seg(O_Q, O_KV), qg_ref[...]).astype(BF16)
    q = _dot(qc, wq_ref[...])
    for hh in range(HEADS):
        base = hh * 2 * LANES
        qa = _dot(q[:, base:base + QK_NOPE].astype(BF16), wuk_ref[hh]) * MLA_SCALE
        qp = rope(q[:, base + LANES:base + 2 * LANES]) * MLA_SCALE
        if prompt:
            cols = slice(hh * tm, (hh + 1) * tm)
            q_ref[0:KV_RANK, cols] = qa.T.astype(BF16)
            q_ref[KV_RANK:KW, cols] = qp.T.astype(BF16)
        else:
            q_ref[hh, :, 0:KV_RANK] = qa.astype(BF16)
            q_ref[hh, :, KV_RANK:KW] = qp.astype(BF16)

    kv = _rms(seg(O_KV, O_PE), kvg_ref[...])
    kvc_ref[...] = kv
    kmat_ref[:, 0:KV_RANK] = kv.astype(BF16)
    if prompt:
        extra_ref[...] = kv.T.astype(BF16)
    kp = rope(seg(O_PE, O_U))
    kpe_ref[...] = kp[:, 0:QK_ROPE]
    kmat_ref[:, KV_RANK:KW] = kp.astype(BF16)

    u = _gelu(seg(O_U, O_V))
    v = _rms(_gelu(seg(O_V, D_IN_PAD)), vg_ref[...])
    if not prompt:
        extra_ref[...] = v
    vb = v.astype(BF16)
    row = lax.broadcasted_iota(jnp.int32, (CHUNK, CHUNK), 0)
    col = lax.broadcasted_iota(jnp.int32, (CHUNK, CHUNK), 1)
    for g in range(GROUPS):
        wg = jnp.where(row >= col, wmix_ref[g], 0.0).astype(BF16)
        bg = bmix_ref[:, g:g + 1]
        for c in range(tm // CHUNK):
            rs = slice(c * CHUNK, (c + 1) * CHUNK)
            cs = slice(g * GW, (g + 1) * GW)
            cm_sc[rs, cs] = u[rs, cs] * (_dot(wg, vb[rs, cs]) + bg)
    cm_ref[...] = _rms(cm_sc[...], cmg_ref[...]).astype(BF16)


def _proj(x, tabs, tab_blocks, wmix, bmix, w, *, tm, prompt):
    rows, d = x.shape
    n = rows // tm
    cos_t, s1_t, s2_t = tabs
    tab_spec = pl.BlockSpec((tm, LANES), lambda i: (i % tab_blocks, 0))
    row_spec = lambda width: pl.BlockSpec((tm, width), lambda i: (i, 0))
    in_specs = [
        row_spec(d), _const_spec((1, d)), _const_spec((d, D_IN_PAD)), _const_spec((1, Q_RANK)),
        _const_spec((Q_RANK, HEADS * 2 * LANES)), _const_spec((1, KV_RANK)),
        _const_spec((HEADS, QK_NOPE, KV_RANK)), _const_spec((1, CW)),
        tab_spec, tab_spec, tab_spec,
        _const_spec((GROUPS, CHUNK, CHUNK)), _const_spec((CHUNK, GROUPS)), _const_spec((1, CW)),
    ]
    if prompt:
        q_shape, q_spec = (KW, HEADS * rows), pl.BlockSpec((KW, HEADS * tm), lambda i: (0, i))
        extra_shape = jax.ShapeDtypeStruct((KV_RANK, rows), BF16)
        extra_spec = pl.BlockSpec((KV_RANK, tm), lambda i: (0, i))
    else:
        q_shape, q_spec = (HEADS, rows, KW), pl.BlockSpec((HEADS, tm, KW), lambda i: (0, i, 0))
        extra_shape, extra_spec = jax.ShapeDtypeStruct((rows, CW), F32), row_spec(CW)
    out_shape = [
        jax.ShapeDtypeStruct(q_shape, BF16),
        jax.ShapeDtypeStruct((rows, KW), BF16),
        jax.ShapeDtypeStruct((rows, KV_RANK), F32),
        jax.ShapeDtypeStruct((rows, QK_ROPE), F32),
        jax.ShapeDtypeStruct((rows, CW), BF16),
        extra_shape,
    ]
    out_specs = [q_spec, row_spec(KW), row_spec(KV_RANK), row_spec(QK_ROPE), row_spec(CW), extra_spec]
    return pl.pallas_call(
        functools.partial(_proj_kernel, prompt=prompt),
        grid=(n,), in_specs=in_specs, out_specs=out_specs, out_shape=out_shape,
        scratch_shapes=[pltpu.VMEM((tm, CW), F32)],
        compiler_params=_params(("arbitrary",)),
        name="proj_prompt" if prompt else "proj_sample",
    )(x, w["gmix"], w["win"], w["qg"], w["wq"], w["kvg"], w["wuk"], w["vg"],
      cos_t, s1_t, s2_t, wmix, bmix, w["cmg"])


def _memkv_kernel(m_ref, g_ref, wk_ref, wv_ref, k_ref, v_ref):
    m = _rms(m_ref[...], g_ref[...]).astype(BF16)
    k_ref[...] = _dot(m, wk_ref[...])
    v_ref[...] = _dot(m, wv_ref[...])


def _memkv(mem, g, wk, wv, *, tm):
    rows, d = mem.shape
    row_spec = lambda width: pl.BlockSpec((tm, width), lambda i: (i, 0))
    return pl.pallas_call(
        _memkv_kernel, grid=(rows // tm,),
        in_specs=[row_spec(d), _const_spec((1, d)), _const_spec((d, MEM_INNER)),
                  _const_spec((d, MEM_INNER))],
        out_specs=[row_spec(MEM_INNER), row_spec(MEM_INNER)],
        out_shape=[jax.ShapeDtypeStruct((rows, MEM_INNER), F32)] * 2,
        compiler_params=_params(("arbitrary",)), name="memkv",
    )(mem, g, wk, wv)


def _online_step(s, v, m, l, acc):
    m_new = jnp.maximum(m, jnp.max(s, axis=-1, keepdims=True))
    p = jnp.exp(s - m_new)
    alpha = jnp.exp(m - m_new)
    return (m_new, alpha * l + jnp.sum(p, axis=-1, keepdims=True),
            alpha * acc + _dot(p.astype(v.dtype), v))


def _online_init(m_sc, l_sc, acc_sc):
    m_sc[...] = jnp.full(m_sc.shape, -jnp.inf, F32)
    l_sc[...] = jnp.zeros(l_sc.shape, F32)
    acc_sc[...] = jnp.zeros(acc_sc.shape, F32)


def _pattn_kernel(qt_ref, k_ref, kvt_ref, o_ref, m_sc, l_sc, acc_sc, *, unroll):
    i = pl.program_id(1)
    heads, tq, _ = o_ref.shape
    cols = heads * tq
    _online_init(m_sc, l_sc, acc_sc)

    def keys_of(j):
        return pl.ds(pl.multiple_of(j * tq, tq), tq)

    def scores(j):
        return _dot(k_ref[keys_of(j), :], qt_ref[...])

    def update(j, st):
        m_prev = m_sc[...]
        m_new = jnp.maximum(m_prev, jnp.max(st, axis=0, keepdims=True))
        p = jnp.exp(st - m_new)
        alpha = jnp.exp(m_prev - m_new)
        l_sc[...] = alpha * l_sc[...] + jnp.sum(p, axis=0, keepdims=True)
        acc_sc[...] = alpha * acc_sc[...] + _dot(kvt_ref[:, keys_of(j)], p.astype(BF16))
        m_sc[...] = m_new

    def block(j, masked):
        st = scores(j)
        if masked:
            key = lax.broadcasted_iota(jnp.int32, (tq, cols), 0)
            query = lax.broadcasted_iota(jnp.int32, (tq, cols), 1) % tq
            st = jnp.where(key <= query, st, -jnp.inf)
        update(j, st)

    def block_group(grp, carry):
        first = grp * unroll
        st_next = scores(first)
        for u in range(unroll):
            st = st_next
            if u + 1 < unroll:
                st_next = scores(first + u + 1)
            update(first + u, st)
        return carry

    def single_block(j, carry):
        block(j, False)
        return carry

    groups = i // unroll
    lax.fori_loop(0, groups, block_group, 0)
    lax.fori_loop(groups * unroll, i, single_block, 0)
    block(i, True)

    ot = acc_sc[...] * (1.0 / l_sc[...])
    for hh in range(heads):
        o_ref[hh] = ot[:, hh * tq:(hh + 1) * tq].T.astype(BF16)


def _pattn(qt, kmat, kvt, *, batch, seq, tq, unroll):
    heads = qt.shape[1] // (batch * seq)
    nq = seq // tq
    return pl.pallas_call(
        functools.partial(_pattn_kernel, unroll=unroll), grid=(batch, nq),
        in_specs=[pl.BlockSpec((KW, heads * tq), lambda b, i: (0, b * nq + i)),
                  pl.BlockSpec((seq, KW), lambda b, i: (b, 0)),
                  pl.BlockSpec((KV_RANK, seq), lambda b, i: (0, b))],
        out_specs=pl.BlockSpec((heads, tq, KV_RANK), lambda b, i: (0, b * nq + i, 0)),
        out_shape=jax.ShapeDtypeStruct((heads, batch * seq, KV_RANK), BF16),
        scratch_shapes=[pltpu.VMEM((1, heads * tq), F32), pltpu.VMEM((1, heads * tq), F32),
                        pltpu.VMEM((KV_RANK, heads * tq), F32)],
        compiler_params=_params(("arbitrary", "arbitrary")), name="pattn",
    )(qt, kmat, kvt)


def _dattn_kernel(pt_ref, q_ref, kn_ref, ckv_hbm, cpe_hbm, o_ref,
                  *scratch, layer, pages_per_chunk, nbuf, new_tokens):
    kvbufs, pebufs = scratch[0:nbuf], scratch[nbuf:2 * nbuf]
    sems, m_sc, l_sc, acc_sc = scratch[2 * nbuf:]
    b = pl.program_id(0)
    c = pl.program_id(1)
    nb = pl.num_programs(0)
    nc = pl.num_programs(1)
    page = kvbufs[0].shape[0] // pages_per_chunk
    pages_per_step = nbuf * pages_per_chunk
    first_page = (b * nc + c) * pages_per_step

    def start_chunk(first, buf):
        for j in range(pages_per_chunk):
            page_id = pt_ref[first + j]
            keys = slice(j * page, (j + 1) * page)
            pltpu.make_async_copy(ckv_hbm.at[layer, page_id], kvbufs[buf].at[keys], sems.at[0, buf]).start()
            pltpu.make_async_copy(cpe_hbm.at[layer, page_id], pebufs[buf].at[:, keys], sems.at[1, buf]).start()

    def wait_chunk(buf):
        pltpu.make_async_copy(kvbufs[buf], kvbufs[buf], sems.at[0, buf]).wait()
        pltpu.make_async_copy(pebufs[buf], pebufs[buf], sems.at[1, buf]).wait()

    q = q_ref[0]
    qa = q[:, 0:KV_RANK].astype(F32)
    qp = q[:, KV_RANK:KV_RANK + QK_ROPE].astype(F32)

    def scores(buf):
        wait_chunk(buf)
        return _dot_nt(qa, kvbufs[buf][...]) + _dot(qp, pebufs[buf][...])

    last_step = b * nc + c + 1 == nb * nc

    @pl.when((b == 0) & (c == 0))
    def _():
        for buf in range(nbuf):
            start_chunk(first_page + buf * pages_per_chunk, buf)

    @pl.when(c == 0)
    def _():
        _online_init(m_sc, l_sc, acc_sc)

    next_first = jnp.where(last_step, 0, first_page + pages_per_step)
    state = (m_sc[...], l_sc[...], acc_sc[...])
    s_next = scores(0)
    for buf in range(nbuf):
        s = s_next
        if buf + 1 < nbuf:
            s_next = scores(buf + 1)
        state = _online_step(s, kvbufs[buf][...], *state)
        start_chunk(next_first + buf * pages_per_chunk, buf)
    m_sc[...], l_sc[...], acc_sc[...] = state

    @pl.when(last_step)
    def _():
        for buf in range(nbuf):
            wait_chunk(buf)

    @pl.when(c == nc - 1)
    def _():
        kn = kn_ref[0]
        sn = _dot_nt(q, kn)
        t = lax.broadcasted_iota(jnp.int32, sn.shape, 0) % new_tokens
        cidx = lax.broadcasted_iota(jnp.int32, sn.shape, 1)
        _, l, acc = _online_step(jnp.where(cidx <= t, sn, -jnp.inf), kn[:, 0:KV_RANK], *state)
        o_ref[0] = (acc * (1.0 / l)).astype(BF16)


def _dattn(page_table, q, knew, cache_kv, cache_pe, *, layer, pages_per_chunk, nbuf, new_tokens):
    nbatch, rows, _ = q.shape
    n_pages = page_table.shape[1]
    page = cache_kv.shape[2]
    nc = n_pages // (nbuf * pages_per_chunk)
    keys = pages_per_chunk * page
    grid_spec = pltpu.PrefetchScalarGridSpec(
        num_scalar_prefetch=1, grid=(nbatch, nc),
        in_specs=[pl.BlockSpec((1, rows, KW), lambda b, c, pt: (b, 0, 0)),
                  pl.BlockSpec((1, LANES, KW), lambda b, c, pt: (b, 0, 0)),
                  pl.BlockSpec(memory_space=pl.ANY), pl.BlockSpec(memory_space=pl.ANY)],
        out_specs=pl.BlockSpec((1, rows, KV_RANK), lambda b, c, pt: (b, 0, 0)),
        scratch_shapes=([pltpu.VMEM((keys, KV_RANK), F32)] * nbuf + [pltpu.VMEM((QK_ROPE, keys), F32)] * nbuf
                        + [pltpu.SemaphoreType.DMA((2, nbuf)),
                           pltpu.VMEM((rows, 1), F32), pltpu.VMEM((rows, 1), F32),
                           pltpu.VMEM((rows, KV_RANK), F32)]))
    return pl.pallas_call(
        functools.partial(_dattn_kernel, layer=layer, pages_per_chunk=pages_per_chunk,
                          nbuf=nbuf, new_tokens=new_tokens),
        grid_spec=grid_spec,
        out_shape=jax.ShapeDtypeStruct((nbatch, rows, KV_RANK), BF16),
        compiler_params=_params(("arbitrary", "arbitrary")), name="dattn",
    )(page_table.reshape(-1), q, knew, cache_kv, cache_pe)


def _merge_kernel(x_ref, o_ref, cm_ref, wuv_ref, ag_ref, wout_ref, gmem_ref, wmq_ref, *rest, fuse_mem):
    if fuse_mem:
        mk_ref, mv_ref, x1_ref, om_ref, attn_sc = rest
    else:
        x1_ref, om_ref, attn_sc = rest
    for hh in range(HEADS):
        attn_sc[:, hh * V_HEAD:(hh + 1) * V_HEAD] = _dot(o_ref[hh], wuv_ref[hh])
    attn_n = _rms(attn_sc[...], ag_ref[...]).astype(BF16)
    mix = _dot(attn_n, wout_ref[0:AW, :]) + _dot(cm_ref[...], wout_ref[AW:AW + CW, :])
    x1 = x_ref[...] + mix
    x1_ref[...] = x1
    qm = _dot(_rms(x1, gmem_ref[...]).astype(BF16), wmq_ref[...]) * MEM_SCALE
    if not fuse_mem:
        om_ref[...] = qm.astype(BF16)
        return
    for hh in range(MEM_HEADS):
        cs = slice(hh * MEM_HD, (hh + 1) * MEM_HD)
        s = _dot_nt(qm[:, cs].astype(BF16), mk_ref[0, :, cs])
        p = jnp.exp(s - jnp.max(s, axis=-1, keepdims=True))
        oh = _dot(p.astype(BF16), mv_ref[0, :, cs])
        om_ref[:, cs] = (oh * (1.0 / jnp.sum(p, axis=-1, keepdims=True))).astype(BF16)


def _merge(x, o_lat, cm, w, mem_kv, *, tm, tiles_per_batch):
    rows, d = x.shape
    fuse_mem = mem_kv is not None
    row_spec = lambda width: pl.BlockSpec((tm, width), lambda i: (i, 0))
    in_specs = [row_spec(d), pl.BlockSpec((HEADS, tm, KV_RANK), lambda i: (0, i, 0)), row_spec(CW),
                _const_spec((HEADS, KV_RANK, V_HEAD)), _const_spec((1, AW)),
                _const_spec((AW + CW, d)), _const_spec((1, d)), _const_spec((d, MEM_INNER))]
    args = [x, o_lat, cm, w["wuv"], w["ag"], w["wout"], w["gmem"], w["wmq"]]
    if fuse_mem:
        mem_len = mem_kv[0].shape[1]
        mem_spec = pl.BlockSpec((1, mem_len, MEM_INNER), lambda i: (i // tiles_per_batch, 0, 0))
        in_specs += [mem_spec, mem_spec]
        args += list(mem_kv)
    return pl.pallas_call(
        functools.partial(_merge_kernel, fuse_mem=fuse_mem), grid=(rows // tm,),
        in_specs=in_specs, out_specs=[row_spec(d), row_spec(MEM_INNER)],
        out_shape=[jax.ShapeDtypeStruct((rows, d), F32), jax.ShapeDtypeStruct((rows, MEM_INNER), BF16)],
        scratch_shapes=[pltpu.VMEM((tm, AW), F32)],
        compiler_params=_params(("arbitrary",)), name="merge_mem" if fuse_mem else "merge",
    )(*args)


def _smem_kernel(q_ref, mk_ref, mv_ref, o_ref):
    rows, mrows = q_ref.shape[1], mk_ref.shape[1]
    qhead = lax.broadcasted_iota(jnp.int32, (rows, mrows), 0) % MEM_HEADS
    mhead = lax.broadcasted_iota(jnp.int32, (rows, mrows), 1) % MEM_HEADS
    same_head = qhead == mhead

    def body(bi, carry):
        s = _dot_nt(q_ref[bi], mk_ref[bi].astype(BF16))
        s = jnp.where(same_head, s, -jnp.inf)
        p = jnp.exp(s - jnp.max(s, axis=-1, keepdims=True))
        o = _dot(p.astype(BF16), mv_ref[bi].astype(BF16))
        o_ref[bi] = (o * (1.0 / jnp.sum(p, axis=-1, keepdims=True))).astype(BF16)
        return carry
    lax.fori_loop(0, q_ref.shape[0], body, 0)


def _smem(q, mk, mv, *, layer, bb):
    nbatch, rows, _ = q.shape
    mrows = mk.shape[1]
    nblk = nbatch // bb
    q_spec = pl.BlockSpec((bb, rows, MEM_HD), lambda i: (i, 0, 0))
    m_spec = pl.BlockSpec((bb, mrows, MEM_HD), lambda i: (layer * nblk + i, 0, 0))
    return pl.pallas_call(
        _smem_kernel, grid=(nblk,), in_specs=[q_spec, m_spec, m_spec], out_specs=q_spec,
        out_shape=jax.ShapeDtypeStruct(q.shape, BF16),
        compiler_params=_params(("arbitrary",)), name="smem",
    )(q, mk, mv)


def _ffn_kernel(x1_ref, om_ref, wmo_ref, gffn_ref, wup_ref, wdn_ref, gfin_ref, y_ref, hf_sc, acc_sc):
    f = pl.program_id(1)

    @pl.when(f == 0)
    def _():
        x2 = x1_ref[...] + _dot(om_ref[...], wmo_ref[...])
        acc_sc[...] = x2
        hf_sc[...] = _rms(x2, gffn_ref[...]).astype(BF16)

    a = jnp.maximum(_dot(hf_sc[...], wup_ref[...]), 0.0)
    acc_sc[...] += _dot((a * a).astype(BF16), wdn_ref[...])

    @pl.when(f == pl.num_programs(1) - 1)
    def _():
        y_ref[...] = _rms(acc_sc[...], gfin_ref[...])


def _ffn(x1, om, w, *, tm, tf):
    rows, d = x1.shape
    d_ff = w["wup"].shape[1]
    row_spec = lambda width: pl.BlockSpec((tm, width), lambda i, f: (i, 0))
    return pl.pallas_call(
        _ffn_kernel, grid=(rows // tm, d_ff // tf),
        in_specs=[row_spec(d), row_spec(MEM_INNER), _const_spec((MEM_INNER, d)), _const_spec((1, d)),
                  pl.BlockSpec((d, tf), lambda i, f: (0, f)), pl.BlockSpec((tf, d), lambda i, f: (f, 0)),
                  _const_spec((1, d))],
        out_specs=row_spec(d), out_shape=jax.ShapeDtypeStruct((rows, d), F32),
        scratch_shapes=[pltpu.VMEM((tm, d), BF16), pltpu.VMEM((tm, d), F32)],
        compiler_params=_params(("arbitrary", "arbitrary")), name="ffn",
    )(x1, om, w["wmo"], w["gffn"], w["wup"], w["wdn"], w["gfin"])


def _rope_tables(pos):
    half = QK_ROPE // 2
    inv = ROPE_THETA ** (-jnp.arange(half, dtype=F32) / half)
    ang = pos.astype(F32)[:, None] * inv[None, :]
    cos, sin, z = jnp.cos(ang), jnp.sin(ang), jnp.zeros_like(ang)
    return (jnp.concatenate([cos, cos, z, z], axis=1),
            jnp.concatenate([-sin, z, z, z], axis=1),
            jnp.concatenate([z, sin, z, z], axis=1))


def _layer_weights(l, norm_mix_g, w_in, q_norm_g, w_q_up, kv_norm_g, w_uk, w_uv, v_norm_g, attn_out_g,
                   cmlp_out_g, w_out, norm_mem_g, w_mem_q, w_mem_o, norm_ffn_g, w_ffn_up, w_ffn_down,
                   final_norm_g):
    d = w_in.shape[1]
    o_pe_end = Q_RANK + KV_RANK + QK_ROPE
    win = jnp.concatenate([w_in[l, :, :o_pe_end], jnp.zeros((d, LANES - QK_ROPE), F32),
                           w_in[l, :, o_pe_end:]], axis=1).astype(BF16)
    wq = w_q_up[l].reshape(Q_RANK, HEADS, QK_NOPE + QK_ROPE)
    wq = jnp.concatenate([wq, jnp.zeros((Q_RANK, HEADS, 2 * LANES - QK_NOPE - QK_ROPE), F32)], axis=2)
    row = lambda g: g.reshape(1, -1)
    return dict(
        gmix=row(norm_mix_g[l]), win=win, qg=row(q_norm_g[l]),
        wq=wq.reshape(Q_RANK, HEADS * 2 * LANES).astype(BF16), kvg=row(kv_norm_g[l]),
        wuk=jnp.transpose(w_uk[l], (1, 2, 0)).astype(BF16), vg=row(v_norm_g[l]),
        cmg=row(cmlp_out_g[l]), wuv=jnp.transpose(w_uv[l], (1, 0, 2)).astype(BF16),
        ag=row(attn_out_g[l]), wout=w_out[l].astype(BF16), gmem=row(norm_mem_g[l]),
        wmq=w_mem_q[l].astype(BF16), wmo=w_mem_o[l].astype(BF16), gffn=row(norm_ffn_g[l]),
        wup=w_ffn_up[l].astype(BF16), wdn=w_ffn_down[l].astype(BF16), gfin=row(final_norm_g))


def kernel(x_prompt, x_sample, cache_kv_latent, cache_k_rope, cache_mem_k, cache_mem_v, page_table,
           mem_prompt, norm_mix_g, w_in, q_norm_g, w_q_up, kv_norm_g, w_uk, w_uv, v_norm_g, w_spatial,
           b_spatial, attn_out_g, cmlp_out_g, w_out, norm_mem_g, mem_in_g, w_mem_q, w_mem_k, w_mem_v,
           w_mem_o, norm_ffn_g, w_ffn_up, w_ffn_down, final_norm_g):
    batch, seq, d = x_prompt.shape
    dbatch, t_new, _ = x_sample.shape
    depth = w_in.shape[0]
    assert depth == 1 and seq % CHUNK == 0 and CHUNK % t_new == 0
    l = 0
    mem_len = mem_prompt.shape[1]
    tm = 256
    srows = dbatch * t_new

    w = _layer_weights(l, norm_mix_g, w_in, q_norm_g, w_q_up, kv_norm_g, w_uk, w_uv, v_norm_g,
                       attn_out_g, cmlp_out_g, w_out, norm_mem_g, w_mem_q, w_mem_o, norm_ffn_g,
                       w_ffn_up, w_ffn_down, final_norm_g)

    xp = x_prompt.reshape(batch * seq, d)
    tabs_p = _rope_tables(jnp.arange(seq))
    qt_p, kmat_p, kvc_p, kpe_p, cm_p, kvt_p = _proj(
        xp, tabs_p, seq // tm, w_spatial[l], b_spatial[l].T, w, tm=tm, prompt=True)
    mk_p, mv_p = _memkv(mem_prompt.reshape(batch * mem_len, d), mem_in_g[l].reshape(1, -1),
                        w_mem_k[l].astype(BF16), w_mem_v[l].astype(BF16), tm=tm)
    o_p = _pattn(qt_p, kmat_p, kvt_p, batch=batch, seq=seq, tq=tm, unroll=2)
    mem_kv_p = (mk_p.astype(BF16).reshape(batch, mem_len, MEM_INNER),
                mv_p.astype(BF16).reshape(batch, mem_len, MEM_INNER))
    x1_p, om_p = _merge(xp, o_p, cm_p, w, mem_kv_p, tm=tm, tiles_per_batch=seq // tm)
    y_p = _ffn(x1_p, om_p, w, tm=512, tf=1024)

    xs = x_sample.reshape(srows, d)
    reps = CHUNK // t_new
    tabs_s = _rope_tables(PAST_LEN + jnp.arange(tm) % t_new)
    eye = jnp.eye(reps, dtype=F32)
    wmix_s = jnp.einsum("ab,gts->gatbs", eye, w_spatial[l][:, :t_new, :t_new]).reshape(GROUPS, CHUNK, CHUNK)
    bmix_s = jnp.tile(b_spatial[l][:, :t_new].T, (reps, 1))
    q_s, kmat_s, kvc_s, kpe_s, cm_s, v_s = _proj(xs, tabs_s, 1, wmix_s, bmix_s, w, tm=tm, prompt=False)

    q_sb = jnp.transpose(q_s.reshape(HEADS, dbatch, t_new, KW), (1, 0, 2, 3)).reshape(dbatch, HEADS * t_new, KW)
    knew = jnp.pad(kmat_s.reshape(dbatch, t_new, KW), ((0, 0), (0, LANES - t_new), (0, 0)))
    o_sb = _dattn(page_table, q_sb, knew, cache_kv_latent, jnp.swapaxes(cache_k_rope, 2, 3),
                  layer=l, pages_per_chunk=16, nbuf=4, new_tokens=t_new)
    o_s = jnp.transpose(o_sb.reshape(dbatch, HEADS, t_new, KV_RANK), (1, 0, 2, 3)).reshape(HEADS, srows, KV_RANK)
    x1_s, qm_s = _merge(xs, o_s, cm_s, w, None, tm=tm, tiles_per_batch=1)
    mem_rows = (depth * dbatch, mem_len * MEM_HEADS, MEM_HD)
    om_s = _smem(qm_s.reshape(dbatch, t_new * MEM_HEADS, MEM_HD), cache_mem_k.reshape(mem_rows),
                 cache_mem_v.reshape(mem_rows), layer=l, bb=8)
    y_s = _ffn(x1_s, om_s.reshape(srows, MEM_INNER), w, tm=512, tf=1024)

    mem_shape = (depth, batch, mem_len, MEM_HEADS, MEM_HD)
    return (y_p.reshape(batch, seq, d), y_s.reshape(dbatch, t_new, d),
            kvc_p.reshape(depth, batch, seq, KV_RANK), kpe_p.reshape(depth, batch, seq, QK_ROPE),
            mk_p.reshape(mem_shape), mv_p.reshape(mem_shape),
            kvc_s.reshape(depth, dbatch, t_new, KV_RANK), kpe_s.reshape(depth, dbatch, t_new, QK_ROPE),
            v_s.reshape(depth, dbatch, t_new, GROUPS, GW))
```

```python
import functools

import jax
import jax.numpy as jnp
from jax import lax
from jax.experimental import pallas as pl
from jax.experimental.pallas import tpu as pltpu

F32 = jnp.float32
BF16 = jnp.bfloat16

EPS = 1e-6
ROPE_THETA = 10000.0
PAST_LEN = 16384

LANES = 128
V7X_VMEM_BYTES = 64 * 2**20
VMEM_LIMIT = V7X_VMEM_BYTES - 8 * 2**20

Q_RANK = 512
KV_RANK = 256
QK_NOPE = 128
QK_ROPE = 64
HEADS = 8
V_HEAD = 128
GROUPS = 8
GW = 128
CHUNK = 128
CW = GROUPS * GW
AW = HEADS * V_HEAD
KW = KV_RANK + LANES
MEM_HEADS = 4
MEM_HD = 128
MEM_INNER = MEM_HEADS * MEM_HD
MLA_SCALE = (QK_NOPE + QK_ROPE) ** -0.5
MEM_SCALE = MEM_HD ** -0.5

O_Q = 0
O_KV = O_Q + Q_RANK
O_PE = O_KV + KV_RANK
O_U = O_PE + LANES
O_V = O_U + CW
D_IN_PAD = O_V + CW

NT_DIMS = (((1,), (1,)), ((), ()))


def _const_spec(shape):
    return pl.BlockSpec(shape, lambda *_: (0,) * len(shape), pipeline_mode=pl.Buffered(1))


def _params(semantics):
    return pltpu.CompilerParams(dimension_semantics=semantics, vmem_limit_bytes=VMEM_LIMIT)


def _rms(x, g):
    return x * lax.rsqrt(jnp.mean(x * x, axis=-1, keepdims=True) + EPS) * g


def _gelu(x):
    return 0.5 * x * (1.0 + jnp.tanh(0.7978845608028654 * (x + 0.044715 * (x * x * x))))


def _dot(a, b):
    return jnp.dot(a, b, preferred_element_type=F32)


def _dot_nt(a, b):
    return lax.dot_general(a, b, NT_DIMS, preferred_element_type=F32)


def _proj_kernel(x_ref, gmix_ref, win_ref, qg_ref, wq_ref, kvg_ref, wuk_ref, vg_ref,
                 cos_ref, s1_ref, s2_ref, wmix_ref, bmix_ref, cmg_ref,
                 q_ref, kmat_ref, kvc_ref, kpe_ref, cm_ref, extra_ref, cm_sc, *, prompt):
    tm = x_ref.shape[0]
    h = _rms(x_ref[...], gmix_ref[...]).astype(BF16)

    def seg(lo, hi):
        return _dot(h, win_ref[:, lo:hi])

    cosv, s1, s2 = cos_ref[...], s1_ref[...], s2_ref[...]

    def rope(p):
        return (p * cosv + pltpu.roll(p, LANES - QK_ROPE // 2, 1) * s1
                + pltpu.roll(p, QK_ROPE // 2, 1) * s2)

    qc = _rms(seg(O_Q, O_KV), qg_ref[...]).astype(BF16)
    q = _dot(qc, wq_ref[...])
    for hh in range(HEADS):
        base = hh * 2 * LANES
        qa = _dot(q[:, base:base + QK_NOPE].astype(BF16), wuk_ref[hh]) * MLA_SCALE
        qp = rope(q[:, base + LANES:base + 2 * LANES]) * MLA_SCALE
        if prompt:
            cols = slice(hh * tm, (hh + 1) * tm)
            q_ref[0:KV_RANK, cols] = qa.T.astype(BF16)
            q_ref[KV_RANK:KW, cols] = qp.T.astype(BF16)
        else:
            q_ref[hh, :, 0:KV_RANK] = qa.astype(BF16)
            q_ref[hh, :, KV_RANK:KW] = qp.astype(BF16)

    kv = _rms(seg(O_KV, O_PE), kvg_ref[...])
    kvc_ref[...] = kv
    kmat_ref[:, 0:KV_RANK] = kv.astype(BF16)
    if prompt:
        extra_ref[...] = kv.T.astype(BF16)
    kp = rope(seg(O_PE, O_U))
    kpe_ref[...] = kp[:, 0:QK_ROPE]
    kmat_ref[:, KV_RANK:KW] = kp.astype(BF16)

    u = _gelu(seg(O_U, O_V))
    v = _rms(_gelu(seg(O_V, D_IN_PAD)), vg_ref[...])
    if not prompt:
        extra_ref[...] = v
    vb = v.astype(BF16)
    row = lax.broadcasted_iota(jnp.int32, (CHUNK, CHUNK), 0)
    col = lax.broadcasted_iota(jnp.int32, (CHUNK, CHUNK), 1)
    for g in range(GROUPS):
        wg = jnp.where(row >= col, wmix_ref[g], 0.0).astype(BF16)
        bg = bmix_ref[:, g:g + 1]
        for c in range(tm // CHUNK):
            rs = slice(c * CHUNK, (c + 1) * CHUNK)
            cs = slice(g * GW, (g + 1) * GW)
            cm_sc[rs, cs] = u[rs, cs] * (_dot(wg, vb[rs, cs]) + bg)
    cm_ref[...] = _rms(cm_sc[...], cmg_ref[...]).astype(BF16)


def _proj(x, tabs, tab_blocks, wmix, bmix, w, *, tm, prompt):
    rows, d = x.shape
    n = rows // tm
    cos_t, s1_t, s2_t = tabs
    tab_spec = pl.BlockSpec((tm, LANES), lambda i: (i % tab_blocks, 0))
    row_spec = lambda width: pl.BlockSpec((tm, width), lambda i: (i, 0))
    in_specs = [
        row_spec(d), _const_spec((1, d)), _const_spec((d, D_IN_PAD)), _const_spec((1, Q_RANK)),
        _const_spec((Q_RANK, HEADS * 2 * LANES)), _const_spec((1, KV_RANK)),
        _const_spec((HEADS, QK_NOPE, KV_RANK)), _const_spec((1, CW)),
        tab_spec, tab_spec, tab_spec,
        _const_spec((GROUPS, CHUNK, CHUNK)), _const_spec((CHUNK, GROUPS)), _const_spec((1, CW)),
    ]
    if prompt:
        q_shape, q_spec = (KW, HEADS * rows), pl.BlockSpec((KW, HEADS * tm), lambda i: (0, i))
        extra_shape = jax.ShapeDtypeStruct((KV_RANK, rows), BF16)
        extra_spec = pl.BlockSpec((KV_RANK, tm), lambda i: (0, i))
    else:
        q_shape, q_spec = (HEADS, rows, KW), pl.BlockSpec((HEADS, tm, KW), lambda i: (0, i, 0))
        extra_shape, extra_spec = jax.ShapeDtypeStruct((rows, CW), F32), row_spec(CW)
    out_shape = [
        jax.ShapeDtypeStruct(q_shape, BF16),
        jax.ShapeDtypeStruct((rows, KW), BF16),
        jax.ShapeDtypeStruct((rows, KV_RANK), F32),
        jax.ShapeDtypeStruct((rows, QK_ROPE), F32),
        jax.ShapeDtypeStruct((rows, CW), BF16),
        extra_shape,
    ]
    out_specs = [q_spec, row_spec(KW), row_spec(KV_RANK), row_spec(QK_ROPE), row_spec(CW), extra_spec]
    return pl.pallas_call(
        functools.partial(_proj_kernel, prompt=prompt),
        grid=(n,), in_specs=in_specs, out_specs=out_specs, out_shape=out_shape,
        scratch_shapes=[pltpu.VMEM((tm, CW), F32)],
        compiler_params=_params(("arbitrary",)),
        name="proj_prompt" if prompt else "proj_sample",
    )(x, w["gmix"], w["win"], w["qg"], w["wq"], w["kvg"], w["wuk"], w["vg"],
      cos_t, s1_t, s2_t, wmix, bmix, w["cmg"])


def _memkv_kernel(m_ref, g_ref, wk_ref, wv_ref, k_ref, v_ref):
    m = _rms(m_ref[...], g_ref[...]).astype(BF16)
    k_ref[...] = _dot(m, wk_ref[...])
    v_ref[...] = _dot(m, wv_ref[...])


def _memkv(mem, g, wk, wv, *, tm):
    rows, d = mem.shape
    row_spec = lambda width: pl.BlockSpec((tm, width), lambda i: (i, 0))
    return pl.pallas_call(
        _memkv_kernel, grid=(rows // tm,),
        in_specs=[row_spec(d), _const_spec((1, d)), _const_spec((d, MEM_INNER)),
                  _const_spec((d, MEM_INNER))],
        out_specs=[row_spec(MEM_INNER), row_spec(MEM_INNER)],
        out_shape=[jax.ShapeDtypeStruct((rows, MEM_INNER), F32)] * 2,
        compiler_params=_params(("arbitrary",)), name="memkv",
    )(mem, g, wk, wv)


def _online_step(s, v, m, l, acc):
    m_new = jnp.maximum(m, jnp.max(s, axis=-1, keepdims=True))
    p = jnp.exp(s - m_new)
    alpha = jnp.exp(m - m_new)
    return (m_new, alpha * l + jnp.sum(p, axis=-1, keepdims=True),
            alpha * acc + _dot(p.astype(v.dtype), v))


def _online_init(m_sc, l_sc, acc_sc):
    m_sc[...] = jnp.full(m_sc.shape, -jnp.inf, F32)
    l_sc[...] = jnp.zeros(l_sc.shape, F32)
    acc_sc[...] = jnp.zeros(acc_sc.shape, F32)


def _pattn_kernel(qt_ref, k_ref, kvt_ref, o_ref, m_sc, l_sc, acc_sc, sa_sc, sb_sc):
    i = pl.program_id(1)
    heads, tq, _ = o_ref.shape
    cols = heads * tq
    _online_init(m_sc, l_sc, acc_sc)

    def keys_of(j):
        return pl.ds(pl.multiple_of(j * tq, tq), tq)

    def scores(j, st_ref):
        st_ref[...] = _dot(k_ref[keys_of(j), :], qt_ref[...])

    def update(j, st_ref, masked):
        st = st_ref[...]
        if masked:
            key = lax.broadcasted_iota(jnp.int32, st.shape, 0)
            query = lax.broadcasted_iota(jnp.int32, st.shape, 1) % tq
            st = jnp.where(key <= query, st, -jnp.inf)
        m_prev = m_sc[...]
        m_new = jnp.maximum(m_prev, jnp.max(st, axis=0, keepdims=True))
        p = jnp.exp(st - m_new)
        alpha = jnp.exp(m_prev - m_new)
        l_sc[...] = alpha * l_sc[...] + jnp.sum(p, axis=0, keepdims=True)
        acc_sc[...] = alpha * acc_sc[...] + _dot(kvt_ref[:, keys_of(j)], p.astype(BF16))
        m_sc[...] = m_new

    scores(0, sa_sc)

    def block_pair(pair, carry):
        j = 2 * pair
        scores(j + 1, sb_sc)
        update(j, sa_sc, False)
        scores(j + 2, sa_sc)
        update(j + 1, sb_sc, False)
        return carry

    lax.fori_loop(0, i // 2, block_pair, 0)

    @pl.when(i % 2 == 0)
    def _():
        update(i, sa_sc, True)

    @pl.when(i % 2 == 1)
    def _():
        scores(i, sb_sc)
        update(i - 1, sa_sc, False)
        update(i, sb_sc, True)

    ot = acc_sc[...] * (1.0 / l_sc[...])
    for hh in range(heads):
        o_ref[hh] = ot[:, hh * tq:(hh + 1) * tq].T.astype(BF16)


def _pattn(qt, kmat, kvt, *, batch, seq, tq):
    heads = qt.shape[1] // (batch * seq)
    nq = seq // tq
    return pl.pallas_call(
        _pattn_kernel, grid=(batch, nq),
        in_specs=[pl.BlockSpec((KW, heads * tq), lambda b, i: (0, b * nq + i)),
                  pl.BlockSpec((seq, KW), lambda b, i: (b, 0)),
                  pl.BlockSpec((KV_RANK, seq), lambda b, i: (0, b))],
        out_specs=pl.BlockSpec((heads, tq, KV_RANK), lambda b, i: (0, b * nq + i, 0)),
        out_shape=jax.ShapeDtypeStruct((heads, batch * seq, KV_RANK), BF16),
        scratch_shapes=[pltpu.VMEM((1, heads * tq), F32), pltpu.VMEM((1, heads * tq), F32),
                        pltpu.VMEM((KV_RANK, heads * tq), F32),
                        pltpu.VMEM((tq, heads * tq), F32), pltpu.VMEM((tq, heads * tq), F32)],
        compiler_params=_params(("arbitrary", "arbitrary")), name="pattn",
    )(qt, kmat, kvt)


def _dattn_kernel(pt_ref, q_ref, kn_ref, ckv_hbm, cpe_hbm, o_ref,
                  *scratch, layer, pages_per_chunk, nbuf, new_tokens):
    kvbufs, pebufs = scratch[0:nbuf], scratch[nbuf:2 * nbuf]
    sems, m_sc, l_sc, acc_sc = scratch[2 * nbuf:]
    b = pl.program_id(0)
    c = pl.program_id(1)
    nb = pl.num_programs(0)
    nc = pl.num_programs(1)
    page = kvbufs[0].shape[0] // pages_per_chunk
    pages_per_step = nbuf * pages_per_chunk
    first_page = (b * nc + c) * pages_per_step

    def start_chunk(first, buf):
        for j in range(pages_per_chunk):
            page_id = pt_ref[first + j]
            keys = slice(j * page, (j + 1) * page)
            pltpu.make_async_copy(ckv_hbm.at[layer, page_id], kvbufs[buf].at[keys], sems.at[0, buf]).start()
            pltpu.make_async_copy(cpe_hbm.at[layer, page_id], pebufs[buf].at[:, keys], sems.at[1, buf]).start()

    def wait_chunk(buf):
        pltpu.make_async_copy(kvbufs[buf], kvbufs[buf], sems.at[0, buf]).wait()
        pltpu.make_async_copy(pebufs[buf], pebufs[buf], sems.at[1, buf]).wait()

    q = q_ref[0]
    qa = q[:, 0:KV_RANK].astype(F32)
    qp = q[:, KV_RANK:KV_RANK + QK_ROPE].astype(F32)

    def scores(buf):
        wait_chunk(buf)
        return _dot_nt(qa, kvbufs[buf][...]) + _dot(qp, pebufs[buf][...])

    last_step = b * nc + c + 1 == nb * nc

    @pl.when((b == 0) & (c == 0))
    def _():
        for buf in range(nbuf):
            start_chunk(first_page + buf * pages_per_chunk, buf)

    @pl.when(c == 0)
    def _():
        _online_init(m_sc, l_sc, acc_sc)

    next_first = jnp.where(last_step, 0, first_page + pages_per_step)
    state = (m_sc[...], l_sc[...], acc_sc[...])
    s_next = scores(0)
    for buf in range(nbuf):
        s = s_next
        if buf + 1 < nbuf:
            s_next = scores(buf + 1)
        state = _online_step(s, kvbufs[buf][...], *state)
        start_chunk(next_first + buf * pages_per_chunk, buf)
    m_sc[...], l_sc[...], acc_sc[...] = state

    @pl.when(last_step)
    def _():
        for buf in range(nbuf):
            wait_chunk(buf)

    @pl.when(c == nc - 1)
    def _():
        kn = kn_ref[0]
        sn = _dot_nt(q, kn)
        t = lax.broadcasted_iota(jnp.int32, sn.shape, 0) % new_tokens
        cidx = lax.broadcasted_iota(jnp.int32, sn.shape, 1)
        _, l, acc = _online_step(jnp.where(cidx <= t, sn, -jnp.inf), kn[:, 0:KV_RANK], *state)
        o_ref[0] = (acc * (1.0 / l)).astype(BF16)


def _dattn(page_table, q, knew, cache_kv, cache_pe, *, layer, pages_per_chunk, nbuf, new_tokens):
    nbatch, rows, _ = q.shape
    n_pages = page_table.shape[1]
    page = cache_kv.shape[2]
    nc = n_pages // (nbuf * pages_per_chunk)
    keys = pages_per_chunk * page
    grid_spec = pltpu.PrefetchScalarGridSpec(
        num_scalar_prefetch=1, grid=(nbatch, nc),
        in_specs=[pl.BlockSpec((1, rows, KW), lambda b, c, pt: (b, 0, 0)),
                  pl.BlockSpec((1, LANES, KW), lambda b, c, pt: (b, 0, 0)),
                  pl.BlockSpec(memory_space=pl.ANY), pl.BlockSpec(memory_space=pl.ANY)],
        out_specs=pl.BlockSpec((1, rows, KV_RANK), lambda b, c, pt: (b, 0, 0)),
        scratch_shapes=([pltpu.VMEM((keys, KV_RANK), F32)] * nbuf + [pltpu.VMEM((QK_ROPE, keys), F32)] * nbuf
                        + [pltpu.SemaphoreType.DMA((2, nbuf)),
                           pltpu.VMEM((rows, 1), F32), pltpu.VMEM((rows, 1), F32),
                           pltpu.VMEM((rows, KV_RANK), F32)]))
    return pl.pallas_call(
        functools.partial(_dattn_kernel, layer=layer, pages_per_chunk=pages_per_chunk,
                          nbuf=nbuf, new_tokens=new_tokens),
        grid_spec=grid_spec,
        out_shape=jax.ShapeDtypeStruct((nbatch, rows, KV_RANK), BF16),
        compiler_params=_params(("arbitrary", "arbitrary")), name="dattn",
    )(page_table.reshape(-1), q, knew, cache_kv, cache_pe)


def _merge_kernel(x_ref, o_ref, cm_ref, wuv_ref, ag_ref, wout_ref, gmem_ref, wmq_ref, *rest, fuse_mem):
    if fuse_mem:
        mk_ref, mv_ref, x1_ref, om_ref, attn_sc = rest
    else:
        x1_ref, om_ref, attn_sc = rest
    for hh in range(HEADS):
        attn_sc[:, hh * V_HEAD:(hh + 1) * V_HEAD] = _dot(o_ref[hh], wuv_ref[hh])
    attn_n = _rms(attn_sc[...], ag_ref[...]).astype(BF16)
    mix = _dot(attn_n, wout_ref[0:AW, :]) + _dot(cm_ref[...], wout_ref[AW:AW + CW, :])
    x1 = x_ref[...] + mix
    x1_ref[...] = x1
    qm = _dot(_rms(x1, gmem_ref[...]).astype(BF16), wmq_ref[...]) * MEM_SCALE
    if not fuse_mem:
        om_ref[...] = qm.astype(BF16)
        return
    for hh in range(MEM_HEADS):
        cs = slice(hh * MEM_HD, (hh + 1) * MEM_HD)
        s = _dot_nt(qm[:, cs].astype(BF16), mk_ref[0, :, cs])
        p = jnp.exp(s - jnp.max(s, axis=-1, keepdims=True))
        oh = _dot(p.astype(BF16), mv_ref[0, :, cs])
        om_ref[:, cs] = (oh * (1.0 / jnp.sum(p, axis=-1, keepdims=True))).astype(BF16)


def _merge(x, o_lat, cm, w, mem_kv, *, tm, tiles_per_batch):
    rows, d = x.shape
    fuse_mem = mem_kv is not None
    row_spec = lambda width: pl.BlockSpec((tm, width), lambda i: (i, 0))
    in_specs = [row_spec(d), pl.BlockSpec((HEADS, tm, KV_RANK), lambda i: (0, i, 0)), row_spec(CW),
                _const_spec((HEADS, KV_RANK, V_HEAD)), _const_spec((1, AW)),
                _const_spec((AW + CW, d)), _const_spec((1, d)), _const_spec((d, MEM_INNER))]
    args = [x, o_lat, cm, w["wuv"], w["ag"], w["wout"], w["gmem"], w["wmq"]]
    if fuse_mem:
        mem_len = mem_kv[0].shape[1]
        mem_spec = pl.BlockSpec((1, mem_len, MEM_INNER), lambda i: (i // tiles_per_batch, 0, 0))
        in_specs += [mem_spec, mem_spec]
        args += list(mem_kv)
    return pl.pallas_call(
        functools.partial(_merge_kernel, fuse_mem=fuse_mem), grid=(rows // tm,),
        in_specs=in_specs, out_specs=[row_spec(d), row_spec(MEM_INNER)],
        out_shape=[jax.ShapeDtypeStruct((rows, d), F32), jax.ShapeDtypeStruct((rows, MEM_INNER), BF16)],
        scratch_shapes=[pltpu.VMEM((tm, AW), F32)],
        compiler_params=_params(("arbitrary",)), name="merge_mem" if fuse_mem else "merge",
    )(*args)


def _smem_kernel(q_ref, mk_ref, mv_ref, o_ref):
    rows, mrows = q_ref.shape[1], mk_ref.shape[1]
    qhead = lax.broadcasted_iota(jnp.int32, (rows, mrows), 0) % MEM_HEADS
    mhead = lax.broadcasted_iota(jnp.int32, (rows, mrows), 1) % MEM_HEADS
    same_head = qhead == mhead

    def body(bi, carry):
        s = _dot_nt(q_ref[bi], mk_ref[bi].astype(BF16))
        s = jnp.where(same_head, s, -jnp.inf)
        p = jnp.exp(s - jnp.max(s, axis=-1, keepdims=True))
        o = _dot(p.astype(BF16), mv_ref[bi].astype(BF16))
        o_ref[bi] = (o * (1.0 / jnp.sum(p, axis=-1, keepdims=True))).astype(BF16)
        return carry
    lax.fori_loop(0, q_ref.shape[0], body, 0)


def _smem(q, mk, mv, *, layer, bb):
    nbatch, rows, _ = q.shape
    mrows = mk.shape[1]
    nblk = nbatch // bb
    q_spec = pl.BlockSpec((bb, rows, MEM_HD), lambda i: (i, 0, 0))
    m_spec = pl.BlockSpec((bb, mrows, MEM_HD), lambda i: (layer * nblk + i, 0, 0))
    return pl.pallas_call(
        _smem_kernel, grid=(nblk,), in_specs=[q_spec, m_spec, m_spec], out_specs=q_spec,
        out_shape=jax.ShapeDtypeStruct(q.shape, BF16),
        compiler_params=_params(("arbitrary",)), name="smem",
    )(q, mk, mv)


def _ffn_kernel(x1_ref, om_ref, wmo_ref, gffn_ref, wup_ref, wdn_ref, gfin_ref, y_ref, hf_sc, acc_sc):
    f = pl.program_id(1)

    @pl.when(f == 0)
    def _():
        x2 = x1_ref[...] + _dot(om_ref[...], wmo_ref[...])
        acc_sc[...] = x2
        hf_sc[...] = _rms(x2, gffn_ref[...]).astype(BF16)

    a = jnp.maximum(_dot(hf_sc[...], wup_ref[...]), 0.0)
    acc_sc[...] += _dot((a * a).astype(BF16), wdn_ref[...])

    @pl.when(f == pl.num_programs(1) - 1)
    def _():
        y_ref[...] = _rms(acc_sc[...], gfin_ref[...])


def _ffn(x1, om, w, *, tm, tf):
    rows, d = x1.shape
    d_ff = w["wup"].shape[1]
    row_spec = lambda width: pl.BlockSpec((tm, width), lambda i, f: (i, 0))
    return pl.pallas_call(
        _ffn_kernel, grid=(rows // tm, d_ff // tf),
        in_specs=[row_spec(d), row_spec(MEM_INNER), _const_spec((MEM_INNER, d)), _const_spec((1, d)),
                  pl.BlockSpec((d, tf), lambda i, f: (0, f)), pl.BlockSpec((tf, d), lambda i, f: (f, 0)),
                  _const_spec((1, d))],
        out_specs=row_spec(d), out_shape=jax.ShapeDtypeStruct((rows, d), F32),
        scratch_shapes=[pltpu.VMEM((tm, d), BF16), pltpu.VMEM((tm, d), F32)],
        compiler_params=_params(("arbitrary", "arbitrary")), name="ffn",
    )(x1, om, w["wmo"], w["gffn"], w["wup"], w["wdn"], w["gfin"])


def _rope_tables(pos):
    half = QK_ROPE // 2
    inv = ROPE_THETA ** (-jnp.arange(half, dtype=F32) / half)
    ang = pos.astype(F32)[:, None] * inv[None, :]
    cos, sin, z = jnp.cos(ang), jnp.sin(ang), jnp.zeros_like(ang)
    return (jnp.concatenate([cos, cos, z, z], axis=1),
            jnp.concatenate([-sin, z, z, z], axis=1),
            jnp.concatenate([z, sin, z, z], axis=1))


def _layer_weights(l, norm_mix_g, w_in, q_norm_g, w_q_up, kv_norm_g, w_uk, w_uv, v_norm_g, attn_out_g,
                   cmlp_out_g, w_out, norm_mem_g, w_mem_q, w_mem_o, norm_ffn_g, w_ffn_up, w_ffn_down,
                   final_norm_g):
    d = w_in.shape[1]
    o_pe_end = Q_RANK + KV_RANK + QK_ROPE
    win = jnp.concatenate([w_in[l, :, :o_pe_end], jnp.zeros((d, LANES - QK_ROPE), F32),
                           w_in[l, :, o_pe_end:]], axis=1).astype(BF16)
    wq = w_q_up[l].reshape(Q_RANK, HEADS, QK_NOPE + QK_ROPE)
    wq = jnp.concatenate([wq, jnp.zeros((Q_RANK, HEADS, 2 * LANES - QK_NOPE - QK_ROPE), F32)], axis=2)
    row = lambda g: g.reshape(1, -1)
    return dict(
        gmix=row(norm_mix_g[l]), win=win, qg=row(q_norm_g[l]),
        wq=wq.reshape(Q_RANK, HEADS * 2 * LANES).astype(BF16), kvg=row(kv_norm_g[l]),
        wuk=jnp.transpose(w_uk[l], (1, 2, 0)).astype(BF16), vg=row(v_norm_g[l]),
        cmg=row(cmlp_out_g[l]), wuv=jnp.transpose(w_uv[l], (1, 0, 2)).astype(BF16),
        ag=row(attn_out_g[l]), wout=w_out[l].astype(BF16), gmem=row(norm_mem_g[l]),
        wmq=w_mem_q[l].astype(BF16), wmo=w_mem_o[l].astype(BF16), gffn=row(norm_ffn_g[l]),
        wup=w_ffn_up[l].astype(BF16), wdn=w_ffn_down[l].astype(BF16), gfin=row(final_norm_g))


def kernel(x_prompt, x_sample, cache_kv_latent, cache_k_rope, cache_mem_k, cache_mem_v, page_table,
           mem_prompt, norm_mix_g, w_in, q_norm_g, w_q_up, kv_norm_g, w_uk, w_uv, v_norm_g, w_spatial,
           b_spatial, attn_out_g, cmlp_out_g, w_out, norm_mem_g, mem_in_g, w_mem_q, w_mem_k, w_mem_v,
           w_mem_o, norm_ffn_g, w_ffn_up, w_ffn_down, final_norm_g):
    batch, seq, d = x_prompt.shape
    dbatch, t_new, _ = x_sample.shape
    depth = w_in.shape[0]
    assert depth == 1 and seq % CHUNK == 0 and CHUNK % t_new == 0
    l = 0
    mem_len = mem_prompt.shape[1]
    tm = 256
    srows = dbatch * t_new

    w = _layer_weights(l, norm_mix_g, w_in, q_norm_g, w_q_up, kv_norm_g, w_uk, w_uv, v_norm_g,
                       attn_out_g, cmlp_out_g, w_out, norm_mem_g, w_mem_q, w_mem_o, norm_ffn_g,
                       w_ffn_up, w_ffn_down, final_norm_g)

    xp = x_prompt.reshape(batch * seq, d)
    tabs_p = _rope_tables(jnp.arange(seq))
    qt_p, kmat_p, kvc_p, kpe_p, cm_p, kvt_p = _proj(
        xp, tabs_p, seq // tm, w_spatial[l], b_spatial[l].T, w, tm=tm, prompt=True)
    mk_p, mv_p = _memkv(mem_prompt.reshape(batch * mem_len, d), mem_in_g[l].reshape(1, -1),
                        w_mem_k[l].astype(BF16), w_mem_v[l].astype(BF16), tm=tm)
    o_p = _pattn(qt_p, kmat_p, kvt_p, batch=batch, seq=seq, tq=tm)
    mem_kv_p = (mk_p.astype(BF16).reshape(batch, mem_len, MEM_INNER),
                mv_p.astype(BF16).reshape(batch, mem_len, MEM_INNER))
    x1_p, om_p = _merge(xp, o_p, cm_p, w, mem_kv_p, tm=tm, tiles_per_batch=seq // tm)
    y_p = _ffn(x1_p, om_p, w, tm=512, tf=1024)

    xs = x_sample.reshape(srows, d)
    reps = CHUNK // t_new
    tabs_s = _rope_tables(PAST_LEN + jnp.arange(tm) % t_new)
    eye = jnp.eye(reps, dtype=F32)
    wmix_s = jnp.einsum("ab,gts->gatbs", eye, w_spatial[l][:, :t_new, :t_new]).reshape(GROUPS, CHUNK, CHUNK)
    bmix_s = jnp.tile(b_spatial[l][:, :t_new].T, (reps, 1))
    q_s, kmat_s, kvc_s, kpe_s, cm_s, v_s = _proj(xs, tabs_s, 1, wmix_s, bmix_s, w, tm=tm, prompt=False)

    q_sb = jnp.transpose(q_s.reshape(HEADS, dbatch, t_new, KW), (1, 0, 2, 3)).reshape(dbatch, HEADS * t_new, KW)
    knew = jnp.pad(kmat_s.reshape(dbatch, t_new, KW), ((0, 0), (0, LANES - t_new), (0, 0)))
    o_sb = _dattn(page_table, q_sb, knew, cache_kv_latent, jnp.swapaxes(cache_k_rope, 2, 3),
                  layer=l, pages_per_chunk=16, nbuf=4, new_tokens=t_new)
    o_s = jnp.transpose(o_sb.reshape(dbatch, HEADS, t_new, KV_RANK), (1, 0, 2, 3)).reshape(HEADS, srows, KV_RANK)
    x1_s, qm_s = _merge(xs, o_s, cm_s, w, None, tm=tm, tiles_per_batch=1)
    mem_rows = (depth * dbatch, mem_len * MEM_HEADS, MEM_HD)
    om_s = _smem(qm_s.reshape(dbatch, t_new * MEM_HEADS, MEM_HD), cache_mem_k.reshape(mem_rows),
                 cache_mem_v.reshape(mem_rows), layer=l, bb=8)
    y_s = _ffn(x1_s, om_s.reshape(srows, MEM_INNER), w, tm=512, tf=1024)

    mem_shape = (depth, batch, mem_len, MEM_HEADS, MEM_HD)
    return (y_p.reshape(batch, seq, d), y_s.reshape(dbatch, t_new, d),
            kvc_p.reshape(depth, batch, seq, KV_RANK), kpe_p.reshape(depth, batch, seq, QK_ROPE),
            mk_p.reshape(mem_shape), mv_p.reshape(mem_shape),
            kvc_s.reshape(depth, dbatch, t_new, KV_RANK), kpe_s.reshape(depth, dbatch, t_new, QK_ROPE),
            v_s.reshape(depth, dbatch, t_new, GROUPS, GW))
```

```python
import functools

import jax
import jax.numpy as jnp
from jax import lax
from jax.experimental import pallas as pl
from jax.experimental.pallas import tpu as pltpu

F32 = jnp.float32
BF16 = jnp.bfloat16

EPS = 1e-6
ROPE_THETA = 10000.0
PAST_LEN = 16384

LANES = 128
V7X_VMEM_BYTES = 64 * 2**20
VMEM_LIMIT = V7X_VMEM_BYTES - 8 * 2**20

Q_RANK = 512
KV_RANK = 256
QK_NOPE = 128
QK_ROPE = 64
HEADS = 8
V_HEAD = 128
GROUPS = 8
GW = 128
CHUNK = 128
CW = GROUPS * GW
AW = HEADS * V_HEAD
KW = KV_RANK + LANES
MEM_HEADS = 4
MEM_HD = 128
MEM_INNER = MEM_HEADS * MEM_HD
MLA_SCALE = (QK_NOPE + QK_ROPE) ** -0.5
MEM_SCALE = MEM_HD ** -0.5

O_Q = 0
O_KV = O_Q + Q_RANK
O_PE = O_KV + KV_RANK
O_U = O_PE + LANES
O_V = O_U + CW
D_IN_PAD = O_V + CW

NT_DIMS = (((1,), (1,)), ((), ()))


def _const_spec(shape):
    return pl.BlockSpec(shape, lambda *_: (0,) * len(shape), pipeline_mode=pl.Buffered(1))


def _params(semantics):
    return pltpu.CompilerParams(dimension_semantics=semantics, vmem_limit_bytes=VMEM_LIMIT)


def _rms(x, g):
    return x * lax.rsqrt(jnp.mean(x * x, axis=-1, keepdims=True) + EPS) * g


def _gelu(x):
    return 0.5 * x * (1.0 + jnp.tanh(0.7978845608028654 * (x + 0.044715 * (x * x * x))))


def _dot(a, b):
    return jnp.dot(a, b, preferred_element_type=F32)


def _dot_nt(a, b):
    return lax.dot_general(a, b, NT_DIMS, preferred_element_type=F32)


def _proj_kernel(x_ref, gmix_ref, win_ref, qg_ref, wq_ref, kvg_ref, wuk_ref, vg_ref,
                 cos_ref, s1_ref, s2_ref, wmix_ref, bmix_ref, cmg_ref,
                 q_ref, kmat_ref, kvc_ref, kpe_ref, cm_ref, extra_ref, cm_sc, *, prompt):
    tm = x_ref.shape[0]
    h = _rms(x_ref[...], gmix_ref[...]).astype(BF16)

    def seg(lo, hi):
        return _dot(h, win_ref[:, lo:hi])

    cosv, s1, s2 = cos_ref[...], s1_ref[...], s2_ref[...]

    def rope(p):
        return (p * cosv + pltpu.roll(p, LANES - QK_ROPE // 2, 1) * s1
                + pltpu.roll(p, QK_ROPE // 2, 1) * s2)

    qc = _rms(seg(O_Q, O_KV), qg_ref[...]).astype(BF16)
    q = _dot(qc, wq_ref[...])
    for hh in range(HEADS):
        base = hh * 2 * LANES
        qa = _dot(q[:, base:base + QK_NOPE].astype(BF16), wuk_ref[hh]) * MLA_SCALE
        qp = rope(q[:, base + LANES:base + 2 * LANES]) * MLA_SCALE
        if prompt:
            cols = slice(hh * tm, (hh + 1) * tm)
            q_ref[0:KV_RANK, cols] = qa.T.astype(BF16)
            q_ref[KV_RANK:KW, cols] = qp.T.astype(BF16)
        else:
            q_ref[hh, :, 0:KV_RANK] = qa.astype(BF16)
            q_ref[hh, :, KV_RANK:KW] = qp.astype(BF16)

    kv = _rms(seg(O_KV, O_PE), kvg_ref[...])
    kvc_ref[...] = kv
    kmat_ref[:, 0:KV_RANK] = kv.astype(BF16)
    if prompt:
        extra_ref[...] = kv.T.astype(BF16)
    kp = rope(seg(O_PE, O_U))
    kpe_ref[...] = kp[:, 0:QK_ROPE]
    kmat_ref[:, KV_RANK:KW] = kp.astype(BF16)

    u = _gelu(seg(O_U, O_V))
    v = _rms(_gelu(seg(O_V, D_IN_PAD)), vg_ref[...])
    if not prompt:
        extra_ref[...] = v
    vb = v.astype(BF16)
    row = lax.broadcasted_iota(jnp.int32, (CHUNK, CHUNK), 0)
    col = lax.broadcasted_iota(jnp.int32, (CHUNK, CHUNK), 1)
    for g in range(GROUPS):
        wg = jnp.where(row >= col, wmix_ref[g], 0.0).astype(BF16)
        bg = bmix_ref[:, g:g + 1]
        for c in range(tm // CHUNK):
            rs = slice(c * CHUNK, (c + 1) * CHUNK)
            cs = slice(g * GW, (g + 1) * GW)
            cm_sc[rs, cs] = u[rs, cs] * (_dot(wg, vb[rs, cs]) + bg)
    cm_ref[...] = _rms(cm_sc[...], cmg_ref[...]).astype(BF16)


def _proj(x, tabs, tab_blocks, wmix, bmix, w, *, tm, prompt):
    rows, d = x.shape
    n = rows // tm
    cos_t, s1_t, s2_t = tabs
    tab_spec = pl.BlockSpec((tm, LANES), lambda i: (i % tab_blocks, 0))
    row_spec = lambda width: pl.BlockSpec((tm, width), lambda i: (i, 0))
    in_specs = [
        row_spec(d), _const_spec((1, d)), _const_spec((d, D_IN_PAD)), _const_spec((1, Q_RANK)),
        _const_spec((Q_RANK, HEADS * 2 * LANES)), _const_spec((1, KV_RANK)),
        _const_spec((HEADS, QK_NOPE, KV_RANK)), _const_spec((1, CW)),
        tab_spec, tab_spec, tab_spec,
        _const_spec((GROUPS, CHUNK, CHUNK)), _const_spec((CHUNK, GROUPS)), _const_spec((1, CW)),
    ]
    if prompt:
        q_shape, q_spec = (KW, HEADS * rows), pl.BlockSpec((KW, HEADS * tm), lambda i: (0, i))
        extra_shape = jax.ShapeDtypeStruct((KV_RANK, rows), BF16)
        extra_spec = pl.BlockSpec((KV_RANK, tm), lambda i: (0, i))
    else:
        q_shape, q_spec = (HEADS, rows, KW), pl.BlockSpec((HEADS, tm, KW), lambda i: (0, i, 0))
        extra_shape, extra_spec = jax.ShapeDtypeStruct((rows, CW), F32), row_spec(CW)
    out_shape = [
        jax.ShapeDtypeStruct(q_shape, BF16),
        jax.ShapeDtypeStruct((rows, KW), BF16),
        jax.ShapeDtypeStruct((rows, KV_RANK), F32),
        jax.ShapeDtypeStruct((rows, QK_ROPE), F32),
        jax.ShapeDtypeStruct((rows, CW), BF16),
        extra_shape,
    ]
    out_specs = [q_spec, row_spec(KW), row_spec(KV_RANK), row_spec(QK_ROPE), row_spec(CW), extra_spec]
    return pl.pallas_call(
        functools.partial(_proj_kernel, prompt=prompt),
        grid=(n,), in_specs=in_specs, out_specs=out_specs, out_shape=out_shape,
        scratch_shapes=[pltpu.VMEM((tm, CW), F32)],
        compiler_params=_params(("arbitrary",)),
        name="proj_prompt" if prompt else "proj_sample",
    )(x, w["gmix"], w["win"], w["qg"], w["wq"], w["kvg"], w["wuk"], w["vg"],
      cos_t, s1_t, s2_t, wmix, bmix, w["cmg"])


def _memkv_kernel(m_ref, g_ref, wk_ref, wv_ref, k_ref, v_ref):
    m = _rms(m_ref[...], g_ref[...]).astype(BF16)
    k_ref[...] = _dot(m, wk_ref[...])
    v_ref[...] = _dot(m, wv_ref[...])


def _memkv(mem, g, wk, wv, *, tm):
    rows, d = mem.shape
    row_spec = lambda width: pl.BlockSpec((tm, width), lambda i: (i, 0))
    return pl.pallas_call(
        _memkv_kernel, grid=(rows // tm,),
        in_specs=[row_spec(d), _const_spec((1, d)), _const_spec((d, MEM_INNER)),
                  _const_spec((d, MEM_INNER))],
        out_specs=[row_spec(MEM_INNER), row_spec(MEM_INNER)],
        out_shape=[jax.ShapeDtypeStruct((rows, MEM_INNER), F32)] * 2,
        compiler_params=_params(("arbitrary",)), name="memkv",
    )(mem, g, wk, wv)


def _online_step(s, v, m, l, acc):
    m_new = jnp.maximum(m, jnp.max(s, axis=-1, keepdims=True))
    p = jnp.exp(s - m_new)
    alpha = jnp.exp(m - m_new)
    return (m_new, alpha * l + jnp.sum(p, axis=-1, keepdims=True),
            alpha * acc + _dot(p.astype(v.dtype), v))


def _online_init(m_sc, l_sc, acc_sc):
    m_sc[...] = jnp.full(m_sc.shape, -jnp.inf, F32)
    l_sc[...] = jnp.zeros(l_sc.shape, F32)
    acc_sc[...] = jnp.zeros(acc_sc.shape, F32)


def _pattn_kernel(qt_ref, k_ref, kvt_ref, o_ref, m_sc, l_sc, acc_sc, sa_sc, sb_sc):
    i = pl.program_id(1)
    heads, tq, _ = o_ref.shape
    cols = heads * tq
    _online_init(m_sc, l_sc, acc_sc)

    def keys_of(j):
        return pl.ds(pl.multiple_of(j * tq, tq), tq)

    def scores(j, st_ref):
        st_ref[...] = _dot(k_ref[keys_of(j), :], qt_ref[...])

    def update(j, st_ref, masked):
        st = st_ref[...]
        if masked:
            key = lax.broadcasted_iota(jnp.int32, st.shape, 0)
            query = lax.broadcasted_iota(jnp.int32, st.shape, 1) % tq
            st = jnp.where(key <= query, st, -jnp.inf)
        m_prev = m_sc[...]
        m_new = jnp.maximum(m_prev, jnp.max(st, axis=0, keepdims=True))
        p = jnp.exp(st - m_new)
        alpha = jnp.exp(m_prev - m_new)
        l_sc[...] = alpha * l_sc[...] + jnp.sum(p, axis=0, keepdims=True)
        acc_sc[...] = alpha * acc_sc[...] + _dot(kvt_ref[:, keys_of(j)], p.astype(BF16))
        m_sc[...] = m_new

    scores(0, sa_sc)

    def block_pair(pair, carry):
        j = 2 * pair
        scores(j + 1, sb_sc)
        update(j, sa_sc, False)
        scores(j + 2, sa_sc)
        update(j + 1, sb_sc, False)
        return carry

    lax.fori_loop(0, i // 2, block_pair, 0)

    @pl.when(i % 2 == 0)
    def _():
        update(i, sa_sc, True)

    @pl.when(i % 2 == 1)
    def _():
        scores(i, sb_sc)
        update(i - 1, sa_sc, False)
        update(i, sb_sc, True)

    ot = acc_sc[...] * (1.0 / l_sc[...])
    for hh in range(heads):
        o_ref[hh] = ot[:, hh * tq:(hh + 1) * tq].T.astype(BF16)


def _pattn(qt, kmat, kvt, *, batch, seq, tq):
    heads = qt.shape[1] // (batch * seq)
    nq = seq // tq
    return pl.pallas_call(
        _pattn_kernel, grid=(batch, nq),
        in_specs=[pl.BlockSpec((KW, heads * tq), lambda b, i: (0, b * nq + i)),
                  pl.BlockSpec((seq, KW), lambda b, i: (b, 0)),
                  pl.BlockSpec((KV_RANK, seq), lambda b, i: (0, b))],
        out_specs=pl.BlockSpec((heads, tq, KV_RANK), lambda b, i: (0, b * nq + i, 0)),
        out_shape=jax.ShapeDtypeStruct((heads, batch * seq, KV_RANK), BF16),
        scratch_shapes=[pltpu.VMEM((1, heads * tq), F32), pltpu.VMEM((1, heads * tq), F32),
                        pltpu.VMEM((KV_RANK, heads * tq), F32),
                        pltpu.VMEM((tq, heads * tq), F32), pltpu.VMEM((tq, heads * tq), F32)],
        compiler_params=_params(("arbitrary", "arbitrary")), name="pattn",
    )(qt, kmat, kvt)


def _dattn_kernel(pt_ref, q_ref, kn_ref, ckv_hbm, cpe_hbm, o_ref,
                  *scratch, layer, pages_per_chunk, nbuf, new_tokens):
    kvbufs, pebufs = scratch[0:nbuf], scratch[nbuf:2 * nbuf]
    sems, m_sc, l_sc, acc_sc = scratch[2 * nbuf:]
    b = pl.program_id(0)
    c = pl.program_id(1)
    nb = pl.num_programs(0)
    nc = pl.num_programs(1)
    page = kvbufs[0].shape[0] // pages_per_chunk
    pages_per_step = nbuf * pages_per_chunk
    first_page = (b * nc + c) * pages_per_step

    def start_chunk(first, buf):
        for j in range(pages_per_chunk):
            page_id = pt_ref[first + j]
            keys = slice(j * page, (j + 1) * page)
            pltpu.make_async_copy(ckv_hbm.at[layer, page_id], kvbufs[buf].at[keys], sems.at[0, buf]).start()
            pltpu.make_async_copy(cpe_hbm.at[layer, page_id], pebufs[buf].at[:, keys], sems.at[1, buf]).start()

    def wait_chunk(buf):
        pltpu.make_async_copy(kvbufs[buf], kvbufs[buf], sems.at[0, buf]).wait()
        pltpu.make_async_copy(pebufs[buf], pebufs[buf], sems.at[1, buf]).wait()

    q = q_ref[0]
    qa = q[:, 0:KV_RANK].astype(F32)
    qp = q[:, KV_RANK:KV_RANK + QK_ROPE].astype(F32)

    def scores(buf):
        wait_chunk(buf)
        return _dot_nt(qa, kvbufs[buf][...]) + _dot(qp, pebufs[buf][...])

    last_step = b * nc + c + 1 == nb * nc

    @pl.when((b == 0) & (c == 0))
    def _():
        for buf in range(nbuf):
            start_chunk(first_page + buf * pages_per_chunk, buf)

    @pl.when(c == 0)
    def _():
        _online_init(m_sc, l_sc, acc_sc)

    next_first = jnp.where(last_step, 0, first_page + pages_per_step)
    state = (m_sc[...], l_sc[...], acc_sc[...])
    s_next = scores(0)
    for buf in range(nbuf):
        s = s_next
        if buf + 1 < nbuf:
            s_next = scores(buf + 1)
        state = _online_step(s, kvbufs[buf][...], *state)
        start_chunk(next_first + buf * pages_per_chunk, buf)
    m_sc[...], l_sc[...], acc_sc[...] = state

    @pl.when(last_step)
    def _():
        for buf in range(nbuf):
            wait_chunk(buf)

    @pl.when(c == nc - 1)
    def _():
        kn = kn_ref[0]
        sn = _dot_nt(q, kn)
        t = lax.broadcasted_iota(jnp.int32, sn.shape, 0) % new_tokens
        cidx = lax.broadcasted_iota(jnp.int32, sn.shape, 1)
        _, l, acc = _online_step(jnp.where(cidx <= t, sn, -jnp.inf), kn[:, 0:KV_RANK], *state)
        o_ref[0] = (acc * (1.0 / l)).astype(BF16)


def _dattn(page_table, q, knew, cache_kv, cache_pe, *, layer, pages_per_chunk, nbuf, new_tokens):
    nbatch, rows, _ = q.shape
    n_pages = page_table.shape[1]
    page = cache_kv.shape[2]
    nc = n_pages // (nbuf * pages_per_chunk)
    keys = pages_per_chunk * page
    grid_spec = pltpu.PrefetchScalarGridSpec(
        num_scalar_prefetch=1, grid=(nbatch, nc),
        in_specs=[pl.BlockSpec((1, rows, KW), lambda b, c, pt: (b, 0, 0)),
                  pl.BlockSpec((1, LANES, KW), lambda b, c, pt: (b, 0, 0)),
                  pl.BlockSpec(memory_space=pl.ANY), pl.BlockSpec(memory_space=pl.ANY)],
        out_specs=pl.BlockSpec((1, rows, KV_RANK), lambda b, c, pt: (b, 0, 0)),
        scratch_shapes=([pltpu.VMEM((keys, KV_RANK), F32)] * nbuf + [pltpu.VMEM((QK_ROPE, keys), F32)] * nbuf
                        + [pltpu.SemaphoreType.DMA((2, nbuf)),
                           pltpu.VMEM((rows, 1), F32), pltpu.VMEM((rows, 1), F32),
                           pltpu.VMEM((rows, KV_RANK), F32)]))
    return pl.pallas_call(
        functools.partial(_dattn_kernel, layer=layer, pages_per_chunk=pages_per_chunk,
                          nbuf=nbuf, new_tokens=new_tokens),
        grid_spec=grid_spec,
        out_shape=jax.ShapeDtypeStruct((nbatch, rows, KV_RANK), BF16),
        compiler_params=_params(("arbitrary", "arbitrary")), name="dattn",
    )(page_table.reshape(-1), q, knew, cache_kv, cache_pe)


def _ffn_dattn_kernel(pt_ref, x1_ref, om_ref, wmo_ref, gffn_ref, wup_ref, wdn_ref, gfin_ref,
                      q_ref, kn_ref, ckv_hbm, cpe_hbm, y_ref, o_ref, hf_sc, facc_sc, *scratch,
                      layer, pages_per_chunk, nbuf, steps_per_batch, new_tokens):
    assert nbuf == 4
    kvbufs, pebufs = scratch[0:nbuf], scratch[nbuf:2 * nbuf]
    sems, m_sc, l_sc, acc_sc = scratch[2 * nbuf:]
    f = pl.program_id(1)
    nf = pl.num_programs(1)
    n = pl.program_id(0) * nf + f
    last_step = n + 1 == pl.num_programs(0) * nf
    c = n % steps_per_batch
    page = kvbufs[0].shape[0] // pages_per_chunk
    pages_per_step = nbuf * pages_per_chunk
    first_page = n * pages_per_step

    def start_chunk(first, buf):
        for j in range(pages_per_chunk):
            page_id = pt_ref[first + j]
            keys = slice(j * page, (j + 1) * page)
            pltpu.make_async_copy(ckv_hbm.at[layer, page_id], kvbufs[buf].at[keys], sems.at[0, buf]).start()
            pltpu.make_async_copy(cpe_hbm.at[layer, page_id], pebufs[buf].at[:, keys], sems.at[1, buf]).start()

    def wait_chunk(buf):
        pltpu.make_async_copy(kvbufs[buf], kvbufs[buf], sems.at[0, buf]).wait()
        pltpu.make_async_copy(pebufs[buf], pebufs[buf], sems.at[1, buf]).wait()

    @pl.when(n == 0)
    def _():
        for buf in range(nbuf):
            start_chunk(first_page + buf * pages_per_chunk, buf)

    @pl.when(c == 0)
    def _():
        _online_init(m_sc, l_sc, acc_sc)

    @pl.when(f == 0)
    def _():
        x2 = x1_ref[...] + _dot(om_ref[...], wmo_ref[...])
        facc_sc[...] = x2
        hf_sc[...] = _rms(x2, gffn_ref[...]).astype(BF16)

    q = q_ref[0]
    qa = q[:, 0:KV_RANK].astype(F32)
    qp = q[:, KV_RANK:KV_RANK + QK_ROPE].astype(F32)

    qrows = q.shape[0]
    qa_pad = jnp.concatenate([qa, jnp.zeros((LANES - qrows, KV_RANK), F32)], axis=0)

    def scores(buf):
        st = _dot_nt(kvbufs[buf][...], qa_pad)
        return st.T[0:qrows] + _dot(qp, pebufs[buf][...])

    next_first = jnp.where(last_step, 0, first_page + pages_per_step)

    def attend(s, buf, state):
        state = _online_step(s, kvbufs[buf][...], *state)
        start_chunk(next_first + buf * pages_per_chunk, buf)
        return state

    half_d = wdn_ref.shape[1] // 2
    state = (m_sc[...], l_sc[...], acc_sc[...])

    wait_chunk(0)
    wait_chunk(1)
    s0 = scores(0)
    s1 = scores(1)
    state = attend(s0, 0, state)
    a = jnp.maximum(_dot(hf_sc[...], wup_ref[...]), 0.0)
    state = attend(s1, 1, state)
    a = (a * a).astype(BF16)

    wait_chunk(2)
    wait_chunk(3)
    s2 = scores(2)
    s3 = scores(3)
    state = attend(s2, 2, state)
    facc_sc[:, 0:half_d] += _dot(a, wdn_ref[:, 0:half_d])
    state = attend(s3, 3, state)
    facc_sc[:, half_d:] += _dot(a, wdn_ref[:, half_d:])
    m_sc[...], l_sc[...], acc_sc[...] = state

    @pl.when(last_step)
    def _():
        for buf in range(nbuf):
            wait_chunk(buf)

    @pl.when(c == steps_per_batch - 1)
    def _():
        kn = kn_ref[0]
        sn = _dot_nt(q, kn)
        t = lax.broadcasted_iota(jnp.int32, sn.shape, 0) % new_tokens
        cidx = lax.broadcasted_iota(jnp.int32, sn.shape, 1)
        _, l, acc = _online_step(jnp.where(cidx <= t, sn, -jnp.inf), kn[:, 0:KV_RANK], *state)
        o_ref[0] = (acc * (1.0 / l)).astype(BF16)

    @pl.when(f == nf - 1)
    def _():
        y_ref[...] = _rms(facc_sc[...], gfin_ref[...])


def _ffn_dattn(x1, om, w, page_table, q, knew, cache_kv, cache_pe, *, tm, tf, layer, pages_per_chunk,
               nbuf, new_tokens):
    rows, d = x1.shape
    d_ff = w["wup"].shape[1]
    nbatch, qrows, _ = q.shape
    n_pages = page_table.shape[1]
    page = cache_kv.shape[2]
    keys = pages_per_chunk * page
    ni, nf = rows // tm, d_ff // tf
    steps_per_batch = n_pages // (nbuf * pages_per_chunk)
    assert ni * nf == nbatch * steps_per_batch
    row_spec = lambda width: pl.BlockSpec((tm, width), lambda i, f, pt: (i, 0))
    batch_spec = lambda r, width: pl.BlockSpec(
        (1, r, width), lambda i, f, pt: ((i * nf + f) // steps_per_batch, 0, 0))
    grid_spec = pltpu.PrefetchScalarGridSpec(
        num_scalar_prefetch=1, grid=(ni, nf),
        in_specs=[row_spec(d), row_spec(MEM_INNER), _const_spec((MEM_INNER, d)), _const_spec((1, d)),
                  pl.BlockSpec((d, tf), lambda i, f, pt: (0, f)),
                  pl.BlockSpec((tf, d), lambda i, f, pt: (f, 0)), _const_spec((1, d)),
                  batch_spec(qrows, KW), batch_spec(LANES, KW),
                  pl.BlockSpec(memory_space=pl.ANY), pl.BlockSpec(memory_space=pl.ANY)],
        out_specs=[row_spec(d), batch_spec(qrows, KV_RANK)],
        scratch_shapes=([pltpu.VMEM((tm, d), BF16), pltpu.VMEM((tm, d), F32)]
                        + [pltpu.VMEM((keys, KV_RANK), F32)] * nbuf + [pltpu.VMEM((QK_ROPE, keys), F32)] * nbuf
                        + [pltpu.SemaphoreType.DMA((2, nbuf)),
                           pltpu.VMEM((qrows, 1), F32), pltpu.VMEM((qrows, 1), F32),
                           pltpu.VMEM((qrows, KV_RANK), F32)]))
    return pl.pallas_call(
        functools.partial(_ffn_dattn_kernel, layer=layer, pages_per_chunk=pages_per_chunk, nbuf=nbuf,
                          steps_per_batch=steps_per_batch, new_tokens=new_tokens),
        grid_spec=grid_spec,
        out_shape=[jax.ShapeDtypeStruct((rows, d), F32), jax.ShapeDtypeStruct((nbatch, qrows, KV_RANK), BF16)],
        compiler_params=_params(("arbitrary", "arbitrary")), name="ffn_dattn",
    )(page_table.reshape(-1), x1, om, w["wmo"], w["gffn"], w["wup"], w["wdn"], w["gfin"],
      q, knew, cache_kv, cache_pe)


def _merge_kernel(x_ref, o_ref, cm_ref, wuv_ref, ag_ref, wout_ref, gmem_ref, wmq_ref, *rest, fuse_mem):
    if fuse_mem:
        mk_ref, mv_ref, x1_ref, om_ref, attn_sc = rest
    else:
        x1_ref, om_ref, attn_sc = rest
    for hh in range(HEADS):
        attn_sc[:, hh * V_HEAD:(hh + 1) * V_HEAD] = _dot(o_ref[hh], wuv_ref[hh])
    attn_n = _rms(attn_sc[...], ag_ref[...]).astype(BF16)
    mix = _dot(attn_n, wout_ref[0:AW, :]) + _dot(cm_ref[...], wout_ref[AW:AW + CW, :])
    x1 = x_ref[...] + mix
    x1_ref[...] = x1
    qm = _dot(_rms(x1, gmem_ref[...]).astype(BF16), wmq_ref[...]) * MEM_SCALE
    if not fuse_mem:
        om_ref[...] = qm.astype(BF16)
        return
    for hh in range(MEM_HEADS):
        cs = slice(hh * MEM_HD, (hh + 1) * MEM_HD)
        s = _dot_nt(qm[:, cs].astype(BF16), mk_ref[0, :, cs])
        p = jnp.exp(s - jnp.max(s, axis=-1, keepdims=True))
        oh = _dot(p.astype(BF16), mv_ref[0, :, cs])
        om_ref[:, cs] = (oh * (1.0 / jnp.sum(p, axis=-1, keepdims=True))).astype(BF16)


def _merge(x, o_lat, cm, w, mem_kv, *, tm, tiles_per_batch):
    rows, d = x.shape
    fuse_mem = mem_kv is not None
    row_spec = lambda width: pl.BlockSpec((tm, width), lambda i: (i, 0))
    in_specs = [row_spec(d), pl.BlockSpec((HEADS, tm, KV_RANK), lambda i: (0, i, 0)), row_spec(CW),
                _const_spec((HEADS, KV_RANK, V_HEAD)), _const_spec((1, AW)),
                _const_spec((AW + CW, d)), _const_spec((1, d)), _const_spec((d, MEM_INNER))]
    args = [x, o_lat, cm, w["wuv"], w["ag"], w["wout"], w["gmem"], w["wmq"]]
    if fuse_mem:
        mem_len = mem_kv[0].shape[1]
        mem_spec = pl.BlockSpec((1, mem_len, MEM_INNER), lambda i: (i // tiles_per_batch, 0, 0))
        in_specs += [mem_spec, mem_spec]
        args += list(mem_kv)
    return pl.pallas_call(
        functools.partial(_merge_kernel, fuse_mem=fuse_mem), grid=(rows // tm,),
        in_specs=in_specs, out_specs=[row_spec(d), row_spec(MEM_INNER)],
        out_shape=[jax.ShapeDtypeStruct((rows, d), F32), jax.ShapeDtypeStruct((rows, MEM_INNER), BF16)],
        scratch_shapes=[pltpu.VMEM((tm, AW), F32)],
        compiler_params=_params(("arbitrary",)), name="merge_mem" if fuse_mem else "merge",
    )(*args)


def _smem_kernel(q_ref, mk_ref, mv_ref, o_ref):
    rows, mrows = q_ref.shape[1], mk_ref.shape[1]
    qhead = lax.broadcasted_iota(jnp.int32, (rows, mrows), 0) % MEM_HEADS
    mhead = lax.broadcasted_iota(jnp.int32, (rows, mrows), 1) % MEM_HEADS
    same_head = qhead == mhead

    def body(bi, carry):
        s = _dot_nt(q_ref[bi], mk_ref[bi].astype(BF16))
        s = jnp.where(same_head, s, -jnp.inf)
        p = jnp.exp(s - jnp.max(s, axis=-1, keepdims=True))
        o = _dot(p.astype(BF16), mv_ref[bi].astype(BF16))
        o_ref[bi] = (o * (1.0 / jnp.sum(p, axis=-1, keepdims=True))).astype(BF16)
        return carry
    lax.fori_loop(0, q_ref.shape[0], body, 0)


def _smem(q, mk, mv, *, layer, bb):
    nbatch, rows, _ = q.shape
    mrows = mk.shape[1]
    nblk = nbatch // bb
    q_spec = pl.BlockSpec((bb, rows, MEM_HD), lambda i: (i, 0, 0))
    m_spec = pl.BlockSpec((bb, mrows, MEM_HD), lambda i: (layer * nblk + i, 0, 0))
    return pl.pallas_call(
        _smem_kernel, grid=(nblk,), in_specs=[q_spec, m_spec, m_spec], out_specs=q_spec,
        out_shape=jax.ShapeDtypeStruct(q.shape, BF16),
        compiler_params=_params(("arbitrary",)), name="smem",
    )(q, mk, mv)


def _ffn_kernel(x1_ref, om_ref, wmo_ref, gffn_ref, wup_ref, wdn_ref, gfin_ref, y_ref, hf_sc, acc_sc):
    f = pl.program_id(1)

    @pl.when(f == 0)
    def _():
        x2 = x1_ref[...] + _dot(om_ref[...], wmo_ref[...])
        acc_sc[...] = x2
        hf_sc[...] = _rms(x2, gffn_ref[...]).astype(BF16)

    a = jnp.maximum(_dot(hf_sc[...], wup_ref[...]), 0.0)
    acc_sc[...] += _dot((a * a).astype(BF16), wdn_ref[...])

    @pl.when(f == pl.num_programs(1) - 1)
    def _():
        y_ref[...] = _rms(acc_sc[...], gfin_ref[...])


def _ffn(x1, om, w, *, tm, tf):
    rows, d = x1.shape
    d_ff = w["wup"].shape[1]
    row_spec = lambda width: pl.BlockSpec((tm, width), lambda i, f: (i, 0))
    return pl.pallas_call(
        _ffn_kernel, grid=(rows // tm, d_ff // tf),
        in_specs=[row_spec(d), row_spec(MEM_INNER), _const_spec((MEM_INNER, d)), _const_spec((1, d)),
                  pl.BlockSpec((d, tf), lambda i, f: (0, f)), pl.BlockSpec((tf, d), lambda i, f: (f, 0)),
                  _const_spec((1, d))],
        out_specs=row_spec(d), out_shape=jax.ShapeDtypeStruct((rows, d), F32),
        scratch_shapes=[pltpu.VMEM((tm, d), BF16), pltpu.VMEM((tm, d), F32)],
        compiler_params=_params(("arbitrary", "arbitrary")), name="ffn",
    )(x1, om, w["wmo"], w["gffn"], w["wup"], w["wdn"], w["gfin"])


def _rope_tables(pos):
    half = QK_ROPE // 2
    inv = ROPE_THETA ** (-jnp.arange(half, dtype=F32) / half)
    ang = pos.astype(F32)[:, None] * inv[None, :]
    cos, sin, z = jnp.cos(ang), jnp.sin(ang), jnp.zeros_like(ang)
    return (jnp.concatenate([cos, cos, z, z], axis=1),
            jnp.concatenate([-sin, z, z, z], axis=1),
            jnp.concatenate([z, sin, z, z], axis=1))


def _layer_weights(l, norm_mix_g, w_in, q_norm_g, w_q_up, kv_norm_g, w_uk, w_uv, v_norm_g, attn_out_g,
                   cmlp_out_g, w_out, norm_mem_g, w_mem_q, w_mem_o, norm_ffn_g, w_ffn_up, w_ffn_down,
                   final_norm_g):
    d = w_in.shape[1]
    o_pe_end = Q_RANK + KV_RANK + QK_ROPE
    win = jnp.concatenate([w_in[l, :, :o_pe_end], jnp.zeros((d, LANES - QK_ROPE), F32),
                           w_in[l, :, o_pe_end:]], axis=1).astype(BF16)
    wq = w_q_up[l].reshape(Q_RANK, HEADS, QK_NOPE + QK_ROPE)
    wq = jnp.concatenate([wq, jnp.zeros((Q_RANK, HEADS, 2 * LANES - QK_NOPE - QK_ROPE), F32)], axis=2)
    row = lambda g: g.reshape(1, -1)
    return dict(
        gmix=row(norm_mix_g[l]), win=win, qg=row(q_norm_g[l]),
        wq=wq.reshape(Q_RANK, HEADS * 2 * LANES).astype(BF16), kvg=row(kv_norm_g[l]),
        wuk=jnp.transpose(w_uk[l], (1, 2, 0)).astype(BF16), vg=row(v_norm_g[l]),
        cmg=row(cmlp_out_g[l]), wuv=jnp.transpose(w_uv[l], (1, 0, 2)).astype(BF16),
        ag=row(attn_out_g[l]), wout=w_out[l].astype(BF16), gmem=row(norm_mem_g[l]),
        wmq=w_mem_q[l].astype(BF16), wmo=w_mem_o[l].astype(BF16), gffn=row(norm_ffn_g[l]),
        wup=w_ffn_up[l].astype(BF16), wdn=w_ffn_down[l].astype(BF16), gfin=row(final_norm_g))


def kernel(x_prompt, x_sample, cache_kv_latent, cache_k_rope, cache_mem_k, cache_mem_v, page_table,
           mem_prompt, norm_mix_g, w_in, q_norm_g, w_q_up, kv_norm_g, w_uk, w_uv, v_norm_g, w_spatial,
           b_spatial, attn_out_g, cmlp_out_g, w_out, norm_mem_g, mem_in_g, w_mem_q, w_mem_k, w_mem_v,
           w_mem_o, norm_ffn_g, w_ffn_up, w_ffn_down, final_norm_g):
    batch, seq, d = x_prompt.shape
    dbatch, t_new, _ = x_sample.shape
    depth = w_in.shape[0]
    assert depth == 1 and seq % CHUNK == 0 and CHUNK % t_new == 0
    l = 0
    mem_len = mem_prompt.shape[1]
    tm = 256
    srows = dbatch * t_new

    w = _layer_weights(l, norm_mix_g, w_in, q_norm_g, w_q_up, kv_norm_g, w_uk, w_uv, v_norm_g,
                       attn_out_g, cmlp_out_g, w_out, norm_mem_g, w_mem_q, w_mem_o, norm_ffn_g,
                       w_ffn_up, w_ffn_down, final_norm_g)

    xp = x_prompt.reshape(batch * seq, d)
    tabs_p = _rope_tables(jnp.arange(seq))
    qt_p, kmat_p, kvc_p, kpe_p, cm_p, kvt_p = _proj(
        xp, tabs_p, seq // tm, w_spatial[l], b_spatial[l].T, w, tm=tm, prompt=True)
    mk_p, mv_p = _memkv(mem_prompt.reshape(batch * mem_len, d), mem_in_g[l].reshape(1, -1),
                        w_mem_k[l].astype(BF16), w_mem_v[l].astype(BF16), tm=tm)
    o_p = _pattn(qt_p, kmat_p, kvt_p, batch=batch, seq=seq, tq=tm)
    mem_kv_p = (mk_p.astype(BF16).reshape(batch, mem_len, MEM_INNER),
                mv_p.astype(BF16).reshape(batch, mem_len, MEM_INNER))
    x1_p, om_p = _merge(xp, o_p, cm_p, w, mem_kv_p, tm=tm, tiles_per_batch=seq // tm)

    xs = x_sample.reshape(srows, d)
    reps = CHUNK // t_new
    tabs_s = _rope_tables(PAST_LEN + jnp.arange(tm) % t_new)
    eye = jnp.eye(reps, dtype=F32)
    wmix_s = jnp.einsum("ab,gts->gatbs", eye, w_spatial[l][:, :t_new, :t_new]).reshape(GROUPS, CHUNK, CHUNK)
    bmix_s = jnp.tile(b_spatial[l][:, :t_new].T, (reps, 1))
    q_s, kmat_s, kvc_s, kpe_s, cm_s, v_s = _proj(xs, tabs_s, 1, wmix_s, bmix_s, w, tm=tm, prompt=False)

    q_sb = jnp.transpose(q_s.reshape(HEADS, dbatch, t_new, KW), (1, 0, 2, 3)).reshape(dbatch, HEADS * t_new, KW)
    knew = jnp.pad(kmat_s.reshape(dbatch, t_new, KW), ((0, 0), (0, LANES - t_new), (0, 0)))
    y_p, o_sb = _ffn_dattn(x1_p, om_p, w, page_table, q_sb, knew, cache_kv_latent,
                           jnp.swapaxes(cache_k_rope, 2, 3), tm=512, tf=512, layer=l,
                           pages_per_chunk=16, nbuf=4, new_tokens=t_new)
    o_s = jnp.transpose(o_sb.reshape(dbatch, HEADS, t_new, KV_RANK), (1, 0, 2, 3)).reshape(HEADS, srows, KV_RANK)
    x1_s, qm_s = _merge(xs, o_s, cm_s, w, None, tm=tm, tiles_per_batch=1)
    mem_rows = (depth * dbatch, mem_len * MEM_HEADS, MEM_HD)
    om_s = _smem(qm_s.reshape(dbatch, t_new * MEM_HEADS, MEM_HD), cache_mem_k.reshape(mem_rows),
                 cache_mem_v.reshape(mem_rows), layer=l, bb=8)
    y_s = _ffn(x1_s, om_s.reshape(srows, MEM_INNER), w, tm=512, tf=1024)

    mem_shape = (depth, batch, mem_len, MEM_HEADS, MEM_HD)
    return (y_p.reshape(batch, seq, d), y_s.reshape(dbatch, t_new, d),
            kvc_p.reshape(depth, batch, seq, KV_RANK), kpe_p.reshape(depth, batch, seq, QK_ROPE),
            mk_p.reshape(mem_shape), mv_p.reshape(mem_shape),
            kvc_s.reshape(depth, dbatch, t_new, KV_RANK), kpe_s.reshape(depth, dbatch, t_new, QK_ROPE),
            v_s.reshape(depth, dbatch, t_new, GROUPS, GW))
```

```python
import functools

import jax
import jax.numpy as jnp
from jax import lax
from jax.experimental import pallas as pl
from jax.experimental.pallas import tpu as pltpu

F32 = jnp.float32
BF16 = jnp.bfloat16

EPS = 1e-6
ROPE_THETA = 10000.0
PAST_LEN = 16384

LANES = 128
V7X_VMEM_BYTES = 64 * 2**20
VMEM_LIMIT = V7X_VMEM_BYTES - 8 * 2**20

Q_RANK = 512
KV_RANK = 256
QK_NOPE = 128
QK_ROPE = 64
HEADS = 8
V_HEAD = 128
GROUPS = 8
GW = 128
CHUNK = 128
CW = GROUPS * GW
AW = HEADS * V_HEAD
KW = KV_RANK + LANES
MEM_HEADS = 4
MEM_HD = 128
MEM_INNER = MEM_HEADS * MEM_HD
MLA_SCALE = (QK_NOPE + QK_ROPE) ** -0.5
MEM_SCALE = MEM_HD ** -0.5

O_Q = 0
O_KV = O_Q + Q_RANK
O_PE = O_KV + KV_RANK
O_U = O_PE + LANES
O_V = O_U + CW
D_IN_PAD = O_V + CW

NT_DIMS = (((1,), (1,)), ((), ()))


def _const_spec(shape):
    return pl.BlockSpec(shape, lambda *_: (0,) * len(shape), pipeline_mode=pl.Buffered(1))


def _params(semantics):
    return pltpu.CompilerParams(dimension_semantics=semantics, vmem_limit_bytes=VMEM_LIMIT)


def _rms(x, g):
    return x * lax.rsqrt(jnp.mean(x * x, axis=-1, keepdims=True) + EPS) * g


def _gelu(x):
    return 0.5 * x * (1.0 + jnp.tanh(0.7978845608028654 * (x + 0.044715 * (x * x * x))))


def _dot(a, b):
    return jnp.dot(a, b, preferred_element_type=F32)


def _dot_nt(a, b):
    return lax.dot_general(a, b, NT_DIMS, preferred_element_type=F32)


def _proj_kernel(x_ref, gmix_ref, win_ref, qg_ref, wq_ref, kvg_ref, wuk_ref, vg_ref,
                 cos_ref, s1_ref, s2_ref, wmix_ref, bmix_ref, cmg_ref, *rest, prompt):
    if prompt:
        wup32_ref, wdn32_ref, q_ref, kmat_ref, kvc_ref, kpe_ref, cm_ref, extra_ref, wup_ref, wdn_ref, cm_sc = rest
        wup_ref[...] = wup32_ref[...].astype(BF16)
        wdn_ref[...] = wdn32_ref[...].astype(BF16)
    else:
        q_ref, kmat_ref, kvc_ref, kpe_ref, cm_ref, extra_ref, cm_sc = rest
    tm = x_ref.shape[0]
    h = _rms(x_ref[...], gmix_ref[...]).astype(BF16)

    def seg(lo, hi):
        return _dot(h, win_ref[:, lo:hi])

    cosv, s1, s2 = cos_ref[...], s1_ref[...], s2_ref[...]

    def rope(p):
        return (p * cosv + pltpu.roll(p, LANES - QK_ROPE // 2, 1) * s1
                + pltpu.roll(p, QK_ROPE // 2, 1) * s2)

    qc = _rms(seg(O_Q, O_KV), qg_ref[...]).astype(BF16)
    q = _dot(qc, wq_ref[...])
    for hh in range(HEADS):
        base = hh * 2 * LANES
        qa = _dot(q[:, base:base + QK_NOPE].astype(BF16), wuk_ref[hh]) * MLA_SCALE
        qp = rope(q[:, base + LANES:base + 2 * LANES]) * MLA_SCALE
        if prompt:
            cols = slice(hh * tm, (hh + 1) * tm)
            q_ref[0:KV_RANK, cols] = qa.T.astype(BF16)
            q_ref[KV_RANK:KW, cols] = qp.T.astype(BF16)
        else:
            q_ref[hh, :, 0:KV_RANK] = qa.astype(BF16)
            q_ref[hh, :, KV_RANK:KW] = qp.astype(BF16)

    kv = _rms(seg(O_KV, O_PE), kvg_ref[...])
    kvc_ref[...] = kv
    kmat_ref[:, 0:KV_RANK] = kv.astype(BF16)
    if prompt:
        extra_ref[...] = kv.T.astype(BF16)
    kp = rope(seg(O_PE, O_U))
    kpe_ref[...] = kp[:, 0:QK_ROPE]
    kmat_ref[:, KV_RANK:KW] = kp.astype(BF16)

    u = _gelu(seg(O_U, O_V))
    v = _rms(_gelu(seg(O_V, D_IN_PAD)), vg_ref[...])
    if not prompt:
        extra_ref[...] = v
    vb = v.astype(BF16)
    row = lax.broadcasted_iota(jnp.int32, (CHUNK, CHUNK), 0)
    col = lax.broadcasted_iota(jnp.int32, (CHUNK, CHUNK), 1)
    for g in range(GROUPS):
        wg = jnp.where(row >= col, wmix_ref[g], 0.0).astype(BF16)
        bg = bmix_ref[:, g:g + 1]
        for c in range(tm // CHUNK):
            rs = slice(c * CHUNK, (c + 1) * CHUNK)
            cs = slice(g * GW, (g + 1) * GW)
            cm_sc[rs, cs] = u[rs, cs] * (_dot(wg, vb[rs, cs]) + bg)
    cm_ref[...] = _rms(cm_sc[...], cmg_ref[...]).astype(BF16)


def _proj(x, tabs, tab_blocks, wmix, bmix, w, ffn_weights=None, *, tm, prompt):
    rows, d = x.shape
    n = rows // tm
    cos_t, s1_t, s2_t = tabs
    tab_spec = pl.BlockSpec((tm, LANES), lambda i: (i % tab_blocks, 0))
    row_spec = lambda width: pl.BlockSpec((tm, width), lambda i: (i, 0))
    in_specs = [
        row_spec(d), _const_spec((1, d)), _const_spec((d, D_IN_PAD)), _const_spec((1, Q_RANK)),
        _const_spec((Q_RANK, HEADS * 2 * LANES)), _const_spec((1, KV_RANK)),
        _const_spec((HEADS, QK_NOPE, KV_RANK)), _const_spec((1, CW)),
        tab_spec, tab_spec, tab_spec,
        _const_spec((GROUPS, CHUNK, CHUNK)), _const_spec((CHUNK, GROUPS)), _const_spec((1, CW)),
    ]
    if prompt:
        q_shape, q_spec = (KW, HEADS * rows), pl.BlockSpec((KW, HEADS * tm), lambda i: (0, i))
        extra_shape = jax.ShapeDtypeStruct((KV_RANK, rows), BF16)
        extra_spec = pl.BlockSpec((KV_RANK, tm), lambda i: (0, i))
    else:
        q_shape, q_spec = (HEADS, rows, KW), pl.BlockSpec((HEADS, tm, KW), lambda i: (0, i, 0))
        extra_shape, extra_spec = jax.ShapeDtypeStruct((rows, CW), F32), row_spec(CW)
    out_shape = [
        jax.ShapeDtypeStruct(q_shape, BF16),
        jax.ShapeDtypeStruct((rows, KW), BF16),
        jax.ShapeDtypeStruct((rows, KV_RANK), F32),
        jax.ShapeDtypeStruct((rows, QK_ROPE), F32),
        jax.ShapeDtypeStruct((rows, CW), BF16),
        extra_shape,
    ]
    out_specs = [q_spec, row_spec(KW), row_spec(KV_RANK), row_spec(QK_ROPE), row_spec(CW), extra_spec]
    args = [x, w["gmix"], w["win"], w["qg"], w["wq"], w["kvg"], w["wuk"], w["vg"],
            cos_t, s1_t, s2_t, wmix, bmix, w["cmg"]]
    if prompt:
        wup32, wdn32 = ffn_weights
        slab = wup32.shape[1] // n
        up_spec = pl.BlockSpec((d, slab), lambda i: (0, i))
        dn_spec = pl.BlockSpec((slab, d), lambda i: (i, 0))
        in_specs += [up_spec, dn_spec]
        args += [wup32, wdn32]
        out_shape += [jax.ShapeDtypeStruct(wup32.shape, BF16), jax.ShapeDtypeStruct(wdn32.shape, BF16)]
        out_specs += [up_spec, dn_spec]
    return pl.pallas_call(
        functools.partial(_proj_kernel, prompt=prompt),
        grid=(n,), in_specs=in_specs, out_specs=out_specs, out_shape=out_shape,
        scratch_shapes=[pltpu.VMEM((tm, CW), F32)],
        compiler_params=_params(("arbitrary",)),
        name="proj_prompt" if prompt else "proj_sample",
    )(*args)


def _memkv_kernel(m_ref, g_ref, wk_ref, wv_ref, k_ref, v_ref):
    m = _rms(m_ref[...], g_ref[...]).astype(BF16)
    k_ref[...] = _dot(m, wk_ref[...])
    v_ref[...] = _dot(m, wv_ref[...])


def _memkv(mem, g, wk, wv, *, tm):
    rows, d = mem.shape
    row_spec = lambda width: pl.BlockSpec((tm, width), lambda i: (i, 0))
    return pl.pallas_call(
        _memkv_kernel, grid=(rows // tm,),
        in_specs=[row_spec(d), _const_spec((1, d)), _const_spec((d, MEM_INNER)),
                  _const_spec((d, MEM_INNER))],
        out_specs=[row_spec(MEM_INNER), row_spec(MEM_INNER)],
        out_shape=[jax.ShapeDtypeStruct((rows, MEM_INNER), F32)] * 2,
        compiler_params=_params(("arbitrary",)), name="memkv",
    )(mem, g, wk, wv)


def _online_step(s, v, m, l, acc):
    m_new = jnp.maximum(m, jnp.max(s, axis=-1, keepdims=True))
    p = jnp.exp(s - m_new)
    alpha = jnp.exp(m - m_new)
    return (m_new, alpha * l + jnp.sum(p, axis=-1, keepdims=True),
            alpha * acc + _dot(p.astype(v.dtype), v))


def _online_init(m_sc, l_sc, acc_sc):
    m_sc[...] = jnp.full(m_sc.shape, -jnp.inf, F32)
    l_sc[...] = jnp.zeros(l_sc.shape, F32)
    acc_sc[...] = jnp.zeros(acc_sc.shape, F32)


def _pattn_kernel(qt_ref, k_ref, kvt_ref, o_ref, m_sc, l_sc, acc_sc, sa_sc, sb_sc):
    i = pl.program_id(1)
    heads, tq, _ = o_ref.shape
    cols = heads * tq
    _online_init(m_sc, l_sc, acc_sc)

    def keys_of(j):
        return pl.ds(pl.multiple_of(j * tq, tq), tq)

    def scores(j, st_ref):
        st_ref[...] = _dot(k_ref[keys_of(j), :], qt_ref[...])

    def update(j, st_ref, masked):
        st = st_ref[...]
        if masked:
            key = lax.broadcasted_iota(jnp.int32, st.shape, 0)
            query = lax.broadcasted_iota(jnp.int32, st.shape, 1) % tq
            st = jnp.where(key <= query, st, -jnp.inf)
        m_prev = m_sc[...]
        m_new = jnp.maximum(m_prev, jnp.max(st, axis=0, keepdims=True))
        p = jnp.exp(st - m_new)
        alpha = jnp.exp(m_prev - m_new)
        l_sc[...] = alpha * l_sc[...] + jnp.sum(p, axis=0, keepdims=True)
        acc_sc[...] = alpha * acc_sc[...] + _dot(kvt_ref[:, keys_of(j)], p.astype(BF16))
        m_sc[...] = m_new

    scores(0, sa_sc)

    def block_pair(pair, carry):
        j = 2 * pair
        scores(j + 1, sb_sc)
        update(j, sa_sc, False)
        scores(j + 2, sa_sc)
        update(j + 1, sb_sc, False)
        return carry

    lax.fori_loop(0, i // 2, block_pair, 0)

    @pl.when(i % 2 == 0)
    def _():
        update(i, sa_sc, True)

    @pl.when(i % 2 == 1)
    def _():
        scores(i, sb_sc)
        update(i - 1, sa_sc, False)
        update(i, sb_sc, True)

    ot = acc_sc[...] * (1.0 / l_sc[...])
    for hh in range(heads):
        o_ref[hh] = ot[:, hh * tq:(hh + 1) * tq].T.astype(BF16)


def _pattn(qt, kmat, kvt, *, batch, seq, tq):
    heads = qt.shape[1] // (batch * seq)
    nq = seq // tq
    return pl.pallas_call(
        _pattn_kernel, grid=(batch, nq),
        in_specs=[pl.BlockSpec((KW, heads * tq), lambda b, i: (0, b * nq + i)),
                  pl.BlockSpec((seq, KW), lambda b, i: (b, 0)),
                  pl.BlockSpec((KV_RANK, seq), lambda b, i: (0, b))],
        out_specs=pl.BlockSpec((heads, tq, KV_RANK), lambda b, i: (0, b * nq + i, 0)),
        out_shape=jax.ShapeDtypeStruct((heads, batch * seq, KV_RANK), BF16),
        scratch_shapes=[pltpu.VMEM((1, heads * tq), F32), pltpu.VMEM((1, heads * tq), F32),
                        pltpu.VMEM((KV_RANK, heads * tq), F32),
                        pltpu.VMEM((tq, heads * tq), F32), pltpu.VMEM((tq, heads * tq), F32)],
        compiler_params=_params(("arbitrary", "arbitrary")), name="pattn",
    )(qt, kmat, kvt)


def _ffn_dattn_kernel(pt_ref, x2_ref, hf_ref, wup_ref, wdn_ref, gfin_ref,
                      q_ref, kn_ref, ckv_hbm, cpe_hbm, y_ref, o_ref, facc_sc, *scratch,
                      layer, pages_per_chunk, nbuf, steps_per_batch, new_tokens):
    assert nbuf == 4
    kvbufs, pebufs = scratch[0:nbuf], scratch[nbuf:2 * nbuf]
    sems, m_sc, l_sc, acc_sc = scratch[2 * nbuf:]
    f = pl.program_id(1)
    nf = pl.num_programs(1)
    n = pl.program_id(0) * nf + f
    last_step = n + 1 == pl.num_programs(0) * nf
    c = n % steps_per_batch
    page = kvbufs[0].shape[0] // pages_per_chunk
    pages_per_step = nbuf * pages_per_chunk
    first_page = n * pages_per_step

    def start_chunk(first, buf):
        for j in range(pages_per_chunk):
            page_id = pt_ref[first + j]
            keys = slice(j * page, (j + 1) * page)
            pltpu.make_async_copy(ckv_hbm.at[layer, page_id], kvbufs[buf].at[keys], sems.at[0, buf]).start()
            pltpu.make_async_copy(cpe_hbm.at[layer, page_id], pebufs[buf].at[:, keys], sems.at[1, buf]).start()

    def wait_chunk(buf):
        pltpu.make_async_copy(kvbufs[buf], kvbufs[buf], sems.at[0, buf]).wait()
        pltpu.make_async_copy(pebufs[buf], pebufs[buf], sems.at[1, buf]).wait()

    @pl.when(n == 0)
    def _():
        _online_init(m_sc, l_sc, acc_sc)
        for buf in range(nbuf):
            start_chunk(first_page + buf * pages_per_chunk, buf)

    @pl.when(f == 0)
    def _():
        facc_sc[...] = x2_ref[...]

    q = q_ref[0]
    qa = q[:, 0:KV_RANK].astype(F32)
    qp = q[:, KV_RANK:KV_RANK + QK_ROPE].astype(F32)

    kn = kn_ref[0]
    sn = _dot_nt(q, kn)
    t = lax.broadcasted_iota(jnp.int32, sn.shape, 0) % new_tokens
    cidx = lax.broadcasted_iota(jnp.int32, sn.shape, 1)
    sn = jnp.where(cidx <= t, sn, -jnp.inf)
    m_new = jnp.max(sn, axis=-1, keepdims=True)
    p_new = jnp.exp(sn - m_new)
    first = c == 0
    state = (jnp.where(first, m_new, m_sc[...]),
             jnp.where(first, jnp.sum(p_new, axis=-1, keepdims=True), l_sc[...]),
             jnp.where(first, _dot(p_new.astype(BF16), kn[:, 0:KV_RANK]), acc_sc[...]))

    qrows = q.shape[0]
    qa_pad = jnp.concatenate([qa, jnp.zeros((LANES - qrows, KV_RANK), F32)], axis=0)

    def scores(buf):
        st = _dot_nt(kvbufs[buf][...], qa_pad)
        return st.T[0:qrows] + _dot(qp, pebufs[buf][...])

    next_first = jnp.where(last_step, 0, first_page + pages_per_step)

    def attend(s, buf, state):
        state = _online_step(s, kvbufs[buf][...], *state)
        start_chunk(next_first + buf * pages_per_chunk, buf)
        return state

    half_d = wdn_ref.shape[1] // 2

    wait_chunk(0)
    wait_chunk(1)
    s0 = scores(0)
    s1 = scores(1)
    state = attend(s0, 0, state)
    a = jnp.maximum(_dot(hf_ref[...], wup_ref[...]), 0.0)
    state = attend(s1, 1, state)
    a = (a * a).astype(BF16)

    wait_chunk(2)
    wait_chunk(3)
    s2 = scores(2)
    s3 = scores(3)
    state = attend(s2, 2, state)
    facc_sc[:, 0:half_d] += _dot(a, wdn_ref[:, 0:half_d])
    state = attend(s3, 3, state)
    facc_sc[:, half_d:] += _dot(a, wdn_ref[:, half_d:])
    m_sc[...], l_sc[...], acc_sc[...] = state

    @pl.when(last_step)
    def _():
        for buf in range(nbuf):
            wait_chunk(buf)

    @pl.when(c == steps_per_batch - 1)
    def _():
        o_ref[0] = (state[2] * (1.0 / state[1])).astype(BF16)

    @pl.when(f == nf - 1)
    def _():
        y_ref[...] = _rms(facc_sc[...], gfin_ref[...])


def _ffn_dattn(x2, hf, wup, wdn, gfin, page_table, q, knew, cache_kv, cache_pe, *, tm, tf, layer,
               pages_per_chunk, nbuf, new_tokens):
    rows, d = x2.shape
    d_ff = wup.shape[1]
    nbatch, qrows, _ = q.shape
    n_pages = page_table.shape[1]
    page = cache_kv.shape[2]
    keys = pages_per_chunk * page
    ni, nf = rows // tm, d_ff // tf
    steps_per_batch = n_pages // (nbuf * pages_per_chunk)
    assert ni * nf == nbatch * steps_per_batch
    row_spec = lambda width: pl.BlockSpec((tm, width), lambda i, f, pt: (i, 0))
    batch_spec = lambda r, width: pl.BlockSpec(
        (1, r, width), lambda i, f, pt: ((i * nf + f) // steps_per_batch, 0, 0))
    grid_spec = pltpu.PrefetchScalarGridSpec(
        num_scalar_prefetch=1, grid=(ni, nf),
        in_specs=[row_spec(d), row_spec(d),
                  pl.BlockSpec((d, tf), lambda i, f, pt: (0, f)),
                  pl.BlockSpec((tf, d), lambda i, f, pt: (f, 0)), _const_spec((1, d)),
                  batch_spec(qrows, KW), batch_spec(LANES, KW),
                  pl.BlockSpec(memory_space=pl.ANY), pl.BlockSpec(memory_space=pl.ANY)],
        out_specs=[row_spec(d), batch_spec(qrows, KV_RANK)],
        scratch_shapes=([pltpu.VMEM((tm, d), F32)]
                        + [pltpu.VMEM((keys, KV_RANK), F32)] * nbuf + [pltpu.VMEM((QK_ROPE, keys), F32)] * nbuf
                        + [pltpu.SemaphoreType.DMA((2, nbuf)),
                           pltpu.VMEM((qrows, 1), F32), pltpu.VMEM((qrows, 1), F32),
                           pltpu.VMEM((qrows, KV_RANK), F32)]))
    return pl.pallas_call(
        functools.partial(_ffn_dattn_kernel, layer=layer, pages_per_chunk=pages_per_chunk, nbuf=nbuf,
                          steps_per_batch=steps_per_batch, new_tokens=new_tokens),
        grid_spec=grid_spec,
        out_shape=[jax.ShapeDtypeStruct((rows, d), F32), jax.ShapeDtypeStruct((nbatch, qrows, KV_RANK), BF16)],
        compiler_params=_params(("arbitrary", "arbitrary")), name="ffn_dattn",
    )(page_table.reshape(-1), x2, hf, wup, wdn, gfin, q, knew, cache_kv, cache_pe)


def _merge_kernel(x_ref, o_ref, cm_ref, wuv_ref, ag_ref, wout_ref, gmem_ref, wmq_ref, *rest, fuse_mem):
    if fuse_mem:
        mk_ref, mv_ref, wmo_ref, gffn_ref, xo_ref, ho_ref, attn_sc, om_sc = rest
    else:
        xo_ref, ho_ref, attn_sc = rest
    for hh in range(HEADS):
        attn_sc[:, hh * V_HEAD:(hh + 1) * V_HEAD] = _dot(o_ref[hh], wuv_ref[hh])
    attn_n = _rms(attn_sc[...], ag_ref[...]).astype(BF16)
    mix = _dot(attn_n, wout_ref[0:AW, :]) + _dot(cm_ref[...], wout_ref[AW:AW + CW, :])
    x1 = x_ref[...] + mix
    qm = _dot(_rms(x1, gmem_ref[...]).astype(BF16), wmq_ref[...]) * MEM_SCALE
    if not fuse_mem:
        xo_ref[...] = x1
        ho_ref[...] = qm.astype(BF16)
        return
    for hh in range(MEM_HEADS):
        cs = slice(hh * MEM_HD, (hh + 1) * MEM_HD)
        s = _dot_nt(qm[:, cs].astype(BF16), mk_ref[0, :, cs])
        p = jnp.exp(s - jnp.max(s, axis=-1, keepdims=True))
        oh = _dot(p.astype(BF16), mv_ref[0, :, cs])
        om_sc[:, cs] = (oh * (1.0 / jnp.sum(p, axis=-1, keepdims=True))).astype(BF16)
    x2 = x1 + _dot(om_sc[...], wmo_ref[...])
    xo_ref[...] = x2
    ho_ref[...] = _rms(x2, gffn_ref[...]).astype(BF16)


def _merge(x, o_lat, cm, w, mem_kv, *, tm, tiles_per_batch):
    rows, d = x.shape
    fuse_mem = mem_kv is not None
    row_spec = lambda width: pl.BlockSpec((tm, width), lambda i: (i, 0))
    in_specs = [row_spec(d), pl.BlockSpec((HEADS, tm, KV_RANK), lambda i: (0, i, 0)), row_spec(CW),
                _const_spec((HEADS, KV_RANK, V_HEAD)), _const_spec((1, AW)),
                _const_spec((AW + CW, d)), _const_spec((1, d)), _const_spec((d, MEM_INNER))]
    args = [x, o_lat, cm, w["wuv"], w["ag"], w["wout"], w["gmem"], w["wmq"]]
    scratch = [pltpu.VMEM((tm, AW), F32)]
    second_width = MEM_INNER
    if fuse_mem:
        mem_len = mem_kv[0].shape[1]
        mem_spec = pl.BlockSpec((1, mem_len, MEM_INNER), lambda i: (i // tiles_per_batch, 0, 0))
        in_specs += [mem_spec, mem_spec, _const_spec((MEM_INNER, d)), _const_spec((1, d))]
        args += list(mem_kv) + [w["wmo"], w["gffn"]]
        scratch.append(pltpu.VMEM((tm, MEM_INNER), BF16))
        second_width = d
    return pl.pallas_call(
        functools.partial(_merge_kernel, fuse_mem=fuse_mem), grid=(rows // tm,),
        in_specs=in_specs, out_specs=[row_spec(d), row_spec(second_width)],
        out_shape=[jax.ShapeDtypeStruct((rows, d), F32), jax.ShapeDtypeStruct((rows, second_width), BF16)],
        scratch_shapes=scratch,
        compiler_params=_params(("arbitrary",)), name="merge_mem" if fuse_mem else "merge",
    )(*args)


def _smem_kernel(q_ref, mk_ref, mv_ref, o_ref):
    rows, mrows = q_ref.shape[1], mk_ref.shape[1]
    qhead = lax.broadcasted_iota(jnp.int32, (rows, mrows), 0) % MEM_HEADS
    mhead = lax.broadcasted_iota(jnp.int32, (rows, mrows), 1) % MEM_HEADS
    same_head = qhead == mhead

    def body(bi, carry):
        s = _dot_nt(q_ref[bi], mk_ref[bi].astype(BF16))
        s = jnp.where(same_head, s, -jnp.inf)
        p = jnp.exp(s - jnp.max(s, axis=-1, keepdims=True))
        o = _dot(p.astype(BF16), mv_ref[bi].astype(BF16))
        o_ref[bi] = (o * (1.0 / jnp.sum(p, axis=-1, keepdims=True))).astype(BF16)
        return carry
    lax.fori_loop(0, q_ref.shape[0], body, 0)


def _smem(q, mk, mv, *, layer, bb):
    nbatch, rows, _ = q.shape
    mrows = mk.shape[1]
    nblk = nbatch // bb
    q_spec = pl.BlockSpec((bb, rows, MEM_HD), lambda i: (i, 0, 0))
    m_spec = pl.BlockSpec((bb, mrows, MEM_HD), lambda i: (layer * nblk + i, 0, 0))
    return pl.pallas_call(
        _smem_kernel, grid=(nblk,), in_specs=[q_spec, m_spec, m_spec], out_specs=q_spec,
        out_shape=jax.ShapeDtypeStruct(q.shape, BF16),
        compiler_params=_params(("arbitrary",)), name="smem",
    )(q, mk, mv)


def _ffn_kernel(x1_ref, om_ref, wmo_ref, gffn_ref, wup_ref, wdn_ref, gfin_ref, y_ref, hf_sc, acc_sc):
    f = pl.program_id(1)

    @pl.when(f == 0)
    def _():
        x2 = x1_ref[...] + _dot(om_ref[...], wmo_ref[...])
        acc_sc[...] = x2
        hf_sc[...] = _rms(x2, gffn_ref[...]).astype(BF16)

    a = jnp.maximum(_dot(hf_sc[...], wup_ref[...]), 0.0)
    acc_sc[...] += _dot((a * a).astype(BF16), wdn_ref[...])

    @pl.when(f == pl.num_programs(1) - 1)
    def _():
        y_ref[...] = _rms(acc_sc[...], gfin_ref[...])


def _ffn(x1, om, w, *, tm, tf):
    rows, d = x1.shape
    d_ff = w["wup"].shape[1]
    row_spec = lambda width: pl.BlockSpec((tm, width), lambda i, f: (i, 0))
    return pl.pallas_call(
        _ffn_kernel, grid=(rows // tm, d_ff // tf),
        in_specs=[row_spec(d), row_spec(MEM_INNER), _const_spec((MEM_INNER, d)), _const_spec((1, d)),
                  pl.BlockSpec((d, tf), lambda i, f: (0, f)), pl.BlockSpec((tf, d), lambda i, f: (f, 0)),
                  _const_spec((1, d))],
        out_specs=row_spec(d), out_shape=jax.ShapeDtypeStruct((rows, d), F32),
        scratch_shapes=[pltpu.VMEM((tm, d), BF16), pltpu.VMEM((tm, d), F32)],
        compiler_params=_params(("arbitrary", "arbitrary")), name="ffn",
    )(x1, om, w["wmo"], w["gffn"], w["wup"], w["wdn"], w["gfin"])


def _rope_tables(pos):
    half = QK_ROPE // 2
    inv = ROPE_THETA ** (-jnp.arange(half, dtype=F32) / half)
    ang = pos.astype(F32)[:, None] * inv[None, :]
    cos, sin, z = jnp.cos(ang), jnp.sin(ang), jnp.zeros_like(ang)
    return (jnp.concatenate([cos, cos, z, z], axis=1),
            jnp.concatenate([-sin, z, z, z], axis=1),
            jnp.concatenate([z, sin, z, z], axis=1))


def _layer_weights(l, norm_mix_g, w_in, q_norm_g, w_q_up, kv_norm_g, w_uk, w_uv, v_norm_g, attn_out_g,
                   cmlp_out_g, w_out, norm_mem_g, w_mem_q, w_mem_o, norm_ffn_g, w_ffn_up, w_ffn_down,
                   final_norm_g):
    d = w_in.shape[1]
    o_pe_end = Q_RANK + KV_RANK + QK_ROPE
    win = jnp.concatenate([w_in[l, :, :o_pe_end], jnp.zeros((d, LANES - QK_ROPE), F32),
                           w_in[l, :, o_pe_end:]], axis=1).astype(BF16)
    wq = w_q_up[l].reshape(Q_RANK, HEADS, QK_NOPE + QK_ROPE)
    wq = jnp.concatenate([wq, jnp.zeros((Q_RANK, HEADS, 2 * LANES - QK_NOPE - QK_ROPE), F32)], axis=2)
    row = lambda g: g.reshape(1, -1)
    return dict(
        gmix=row(norm_mix_g[l]), win=win, qg=row(q_norm_g[l]),
        wq=wq.reshape(Q_RANK, HEADS * 2 * LANES).astype(BF16), kvg=row(kv_norm_g[l]),
        wuk=jnp.transpose(w_uk[l], (1, 2, 0)).astype(BF16), vg=row(v_norm_g[l]),
        cmg=row(cmlp_out_g[l]), wuv=jnp.transpose(w_uv[l], (1, 0, 2)).astype(BF16),
        ag=row(attn_out_g[l]), wout=w_out[l].astype(BF16), gmem=row(norm_mem_g[l]),
        wmq=w_mem_q[l].astype(BF16), wmo=w_mem_o[l].astype(BF16), gffn=row(norm_ffn_g[l]),
        wup32=w_ffn_up[l], wdn32=w_ffn_down[l], gfin=row(final_norm_g))


def kernel(x_prompt, x_sample, cache_kv_latent, cache_k_rope, cache_mem_k, cache_mem_v, page_table,
           mem_prompt, norm_mix_g, w_in, q_norm_g, w_q_up, kv_norm_g, w_uk, w_uv, v_norm_g, w_spatial,
           b_spatial, attn_out_g, cmlp_out_g, w_out, norm_mem_g, mem_in_g, w_mem_q, w_mem_k, w_mem_v,
           w_mem_o, norm_ffn_g, w_ffn_up, w_ffn_down, final_norm_g):
    batch, seq, d = x_prompt.shape
    dbatch, t_new, _ = x_sample.shape
    depth = w_in.shape[0]
    assert depth == 1 and seq % CHUNK == 0 and CHUNK % t_new == 0
    l = 0
    mem_len = mem_prompt.shape[1]
    tm = 256
    srows = dbatch * t_new

    w = _layer_weights(l, norm_mix_g, w_in, q_norm_g, w_q_up, kv_norm_g, w_uk, w_uv, v_norm_g,
                       attn_out_g, cmlp_out_g, w_out, norm_mem_g, w_mem_q, w_mem_o, norm_ffn_g,
                       w_ffn_up, w_ffn_down, final_norm_g)

    xp = x_prompt.reshape(batch * seq, d)
    tabs_p = _rope_tables(jnp.arange(seq))
    qt_p, kmat_p, kvc_p, kpe_p, cm_p, kvt_p, w["wup"], w["wdn"] = _proj(
        xp, tabs_p, seq // tm, w_spatial[l], b_spatial[l].T, w, (w["wup32"], w["wdn32"]),
        tm=tm, prompt=True)
    mk_p, mv_p = _memkv(mem_prompt.reshape(batch * mem_len, d), mem_in_g[l].reshape(1, -1),
                        w_mem_k[l].astype(BF16), w_mem_v[l].astype(BF16), tm=tm)
    o_p = _pattn(qt_p, kmat_p, kvt_p, batch=batch, seq=seq, tq=tm)
    mem_kv_p = (mk_p.astype(BF16).reshape(batch, mem_len, MEM_INNER),
                mv_p.astype(BF16).reshape(batch, mem_len, MEM_INNER))
    x2_p, hf_p = _merge(xp, o_p, cm_p, w, mem_kv_p, tm=tm, tiles_per_batch=seq // tm)

    xs = x_sample.reshape(srows, d)
    reps = CHUNK // t_new
    tabs_s = _rope_tables(PAST_LEN + jnp.arange(tm) % t_new)
    eye = jnp.eye(reps, dtype=F32)
    wmix_s = jnp.einsum("ab,gts->gatbs", eye, w_spatial[l][:, :t_new, :t_new]).reshape(GROUPS, CHUNK, CHUNK)
    bmix_s = jnp.tile(b_spatial[l][:, :t_new].T, (reps, 1))
    q_s, kmat_s, kvc_s, kpe_s, cm_s, v_s = _proj(xs, tabs_s, 1, wmix_s, bmix_s, w, tm=tm, prompt=False)

    q_sb = jnp.transpose(q_s.reshape(HEADS, dbatch, t_new, KW), (1, 0, 2, 3)).reshape(dbatch, HEADS * t_new, KW)
    knew = jnp.pad(kmat_s.reshape(dbatch, t_new, KW), ((0, 0), (0, LANES - t_new), (0, 0)))
    y_p, o_sb = _ffn_dattn(x2_p, hf_p, w["wup"], w["wdn"], w["gfin"], page_table, q_sb, knew,
                           cache_kv_latent, jnp.swapaxes(cache_k_rope, 2, 3), tm=512, tf=512, layer=l,
                           pages_per_chunk=16, nbuf=4, new_tokens=t_new)
    o_s = jnp.transpose(o_sb.reshape(dbatch, HEADS, t_new, KV_RANK), (1, 0, 2, 3)).reshape(HEADS, srows, KV_RANK)
    x1_s, qm_s = _merge(xs, o_s, cm_s, w, None, tm=tm, tiles_per_batch=1)
    mem_rows = (depth * dbatch, mem_len * MEM_HEADS, MEM_HD)
    om_s = _smem(qm_s.reshape(dbatch, t_new * MEM_HEADS, MEM_HD), cache_mem_k.reshape(mem_rows),
                 cache_mem_v.reshape(mem_rows), layer=l, bb=8)
    y_s = _ffn(x1_s, om_s.reshape(srows, MEM_INNER), w, tm=512, tf=1024)

    mem_shape = (depth, batch, mem_len, MEM_HEADS, MEM_HD)
    return (y_p.reshape(batch, seq, d), y_s.reshape(dbatch, t_new, d),
            kvc_p.reshape(depth, batch, seq, KV_RANK), kpe_p.reshape(depth, batch, seq, QK_ROPE),
            mk_p.reshape(mem_shape), mv_p.reshape(mem_shape),
            kvc_s.reshape(depth, dbatch, t_new, KV_RANK), kpe_s.reshape(depth, dbatch, t_new, QK_ROPE),
            v_s.reshape(depth, dbatch, t_new, GROUPS, GW))
```

```python
import functools

import jax
import jax.numpy as jnp
from jax import lax
from jax.experimental import pallas as pl
from jax.experimental.pallas import tpu as pltpu

F32 = jnp.float32
BF16 = jnp.bfloat16

EPS = 1e-6
ROPE_THETA = 10000.0
PAST_LEN = 16384

LANES = 128
V7X_VMEM_BYTES = 64 * 2**20
VMEM_LIMIT = V7X_VMEM_BYTES - 8 * 2**20

Q_RANK = 512
KV_RANK = 256
QK_NOPE = 128
QK_ROPE = 64
HEADS = 8
V_HEAD = 128
GROUPS = 8
GW = 128
CHUNK = 128
CW = GROUPS * GW
AW = HEADS * V_HEAD
KW = KV_RANK + LANES
MEM_HEADS = 4
MEM_HD = 128
MEM_INNER = MEM_HEADS * MEM_HD
MLA_SCALE = (QK_NOPE + QK_ROPE) ** -0.5
MEM_SCALE = MEM_HD ** -0.5

O_Q = 0
O_KV = O_Q + Q_RANK
O_PE = O_KV + KV_RANK
O_U = O_PE + LANES
O_V = O_U + CW
D_IN_PAD = O_V + CW

NT_DIMS = (((1,), (1,)), ((), ()))


def _const_spec(shape):
    return pl.BlockSpec(shape, lambda *_: (0,) * len(shape), pipeline_mode=pl.Buffered(1))


def _params(semantics):
    return pltpu.CompilerParams(dimension_semantics=semantics, vmem_limit_bytes=VMEM_LIMIT)


def _rms(x, g):
    return x * lax.rsqrt(jnp.mean(x * x, axis=-1, keepdims=True) + EPS) * g


def _gelu(x):
    return 0.5 * x * (1.0 + jnp.tanh(0.7978845608028654 * (x + 0.044715 * (x * x * x))))


def _dot(a, b):
    return jnp.dot(a, b, preferred_element_type=F32)


def _dot_nt(a, b):
    return lax.dot_general(a, b, NT_DIMS, preferred_element_type=F32)


def _proj_kernel(x_ref, gmix_ref, win_ref, qg_ref, wq_ref, kvg_ref, wuk_ref, vg_ref,
                 cos_ref, s1_ref, s2_ref, wmix_ref, bmix_ref, cmg_ref, *rest, prompt):
    if prompt:
        wup32_ref, wdn32_ref, q_ref, kmat_ref, kvc_ref, kpe_ref, cm_ref, extra_ref, wup_ref, wdn_ref, cm_sc = rest
        wup_ref[...] = wup32_ref[...].astype(BF16)
        wdn_ref[...] = wdn32_ref[...].astype(BF16)
    else:
        q_ref, kmat_ref, kvc_ref, kpe_ref, cm_ref, extra_ref, cm_sc = rest
    tm = x_ref.shape[0]
    h = _rms(x_ref[...], gmix_ref[...]).astype(BF16)

    def seg(lo, hi):
        return _dot(h, win_ref[:, lo:hi])

    cosv, s1, s2 = cos_ref[...], s1_ref[...], s2_ref[...]

    def rope(p):
        return (p * cosv + pltpu.roll(p, LANES - QK_ROPE // 2, 1) * s1
                + pltpu.roll(p, QK_ROPE // 2, 1) * s2)

    qc = _rms(seg(O_Q, O_KV), qg_ref[...]).astype(BF16)
    q = _dot(qc, wq_ref[...])
    for hh in range(HEADS):
        base = hh * 2 * LANES
        qa = _dot(q[:, base:base + QK_NOPE].astype(BF16), wuk_ref[hh]) * MLA_SCALE
        qp = rope(q[:, base + LANES:base + 2 * LANES]) * MLA_SCALE
        if prompt:
            cols = slice(hh * tm, (hh + 1) * tm)
            q_ref[0:KV_RANK, cols] = qa.T.astype(BF16)
            q_ref[KV_RANK:KW, cols] = qp.T.astype(BF16)
        else:
            q_ref[hh, :, 0:KV_RANK] = qa.astype(BF16)
            q_ref[hh, :, KV_RANK:KW] = qp.astype(BF16)

    kv = _rms(seg(O_KV, O_PE), kvg_ref[...])
    kvc_ref[...] = kv
    kmat_ref[:, 0:KV_RANK] = kv.astype(BF16)
    if prompt:
        extra_ref[...] = kv.T.astype(BF16)
    kp = rope(seg(O_PE, O_U))
    kpe_ref[...] = kp[:, 0:QK_ROPE]
    kmat_ref[:, KV_RANK:KW] = kp.astype(BF16)

    u = _gelu(seg(O_U, O_V))
    v = _rms(_gelu(seg(O_V, D_IN_PAD)), vg_ref[...])
    if not prompt:
        extra_ref[...] = v
    vb = v.astype(BF16)
    row = lax.broadcasted_iota(jnp.int32, (CHUNK, CHUNK), 0)
    col = lax.broadcasted_iota(jnp.int32, (CHUNK, CHUNK), 1)
    for g in range(GROUPS):
        wg = jnp.where(row >= col, wmix_ref[g], 0.0).astype(BF16)
        bg = bmix_ref[:, g:g + 1]
        for c in range(tm // CHUNK):
            rs = slice(c * CHUNK, (c + 1) * CHUNK)
            cs = slice(g * GW, (g + 1) * GW)
            cm_sc[rs, cs] = u[rs, cs] * (_dot(wg, vb[rs, cs]) + bg)
    cm_ref[...] = _rms(cm_sc[...], cmg_ref[...]).astype(BF16)


def _proj(x, tabs, tab_blocks, wmix, bmix, w, ffn_weights=None, *, tm, prompt):
    rows, d = x.shape
    n = rows // tm
    cos_t, s1_t, s2_t = tabs
    tab_spec = pl.BlockSpec((tm, LANES), lambda i: (i % tab_blocks, 0))
    row_spec = lambda width: pl.BlockSpec((tm, width), lambda i: (i, 0))
    in_specs = [
        row_spec(d), _const_spec((1, d)), _const_spec((d, D_IN_PAD)), _const_spec((1, Q_RANK)),
        _const_spec((Q_RANK, HEADS * 2 * LANES)), _const_spec((1, KV_RANK)),
        _const_spec((HEADS, QK_NOPE, KV_RANK)), _const_spec((1, CW)),
        tab_spec, tab_spec, tab_spec,
        _const_spec((GROUPS, CHUNK, CHUNK)), _const_spec((CHUNK, GROUPS)), _const_spec((1, CW)),
    ]
    if prompt:
        q_shape, q_spec = (KW, HEADS * rows), pl.BlockSpec((KW, HEADS * tm), lambda i: (0, i))
        extra_shape = jax.ShapeDtypeStruct((KV_RANK, rows), BF16)
        extra_spec = pl.BlockSpec((KV_RANK, tm), lambda i: (0, i))
    else:
        q_shape, q_spec = (HEADS, rows, KW), pl.BlockSpec((HEADS, tm, KW), lambda i: (0, i, 0))
        extra_shape, extra_spec = jax.ShapeDtypeStruct((rows, CW), F32), row_spec(CW)
    out_shape = [
        jax.ShapeDtypeStruct(q_shape, BF16),
        jax.ShapeDtypeStruct((rows, KW), BF16),
        jax.ShapeDtypeStruct((rows, KV_RANK), F32),
        jax.ShapeDtypeStruct((rows, QK_ROPE), F32),
        jax.ShapeDtypeStruct((rows, CW), BF16),
        extra_shape,
    ]
    out_specs = [q_spec, row_spec(KW), row_spec(KV_RANK), row_spec(QK_ROPE), row_spec(CW), extra_spec]
    args = [x, w["gmix"], w["win"], w["qg"], w["wq"], w["kvg"], w["wuk"], w["vg"],
            cos_t, s1_t, s2_t, wmix, bmix, w["cmg"]]
    if prompt:
        wup32, wdn32 = ffn_weights
        slab = wup32.shape[1] // n
        up_spec = pl.BlockSpec((d, slab), lambda i: (0, i))
        dn_spec = pl.BlockSpec((slab, d), lambda i: (i, 0))
        in_specs += [up_spec, dn_spec]
        args += [wup32, wdn32]
        out_shape += [jax.ShapeDtypeStruct(wup32.shape, BF16), jax.ShapeDtypeStruct(wdn32.shape, BF16)]
        out_specs += [up_spec, dn_spec]
    return pl.pallas_call(
        functools.partial(_proj_kernel, prompt=prompt),
        grid=(n,), in_specs=in_specs, out_specs=out_specs, out_shape=out_shape,
        scratch_shapes=[pltpu.VMEM((tm, CW), F32)],
        compiler_params=_params(("arbitrary",)),
        name="proj_prompt" if prompt else "proj_sample",
    )(*args)


def _memkv_kernel(m_ref, g_ref, wk_ref, wv_ref, k_ref, v_ref):
    m = _rms(m_ref[...], g_ref[...]).astype(BF16)
    k_ref[...] = _dot(m, wk_ref[...])
    v_ref[...] = _dot(m, wv_ref[...])


def _memkv(mem, g, wk, wv, *, tm):
    rows, d = mem.shape
    row_spec = lambda width: pl.BlockSpec((tm, width), lambda i: (i, 0))
    return pl.pallas_call(
        _memkv_kernel, grid=(rows // tm,),
        in_specs=[row_spec(d), _const_spec((1, d)), _const_spec((d, MEM_INNER)),
                  _const_spec((d, MEM_INNER))],
        out_specs=[row_spec(MEM_INNER), row_spec(MEM_INNER)],
        out_shape=[jax.ShapeDtypeStruct((rows, MEM_INNER), F32)] * 2,
        compiler_params=_params(("arbitrary",)), name="memkv",
    )(mem, g, wk, wv)


def _online_step(s, v, m, l, acc):
    m_new = jnp.maximum(m, jnp.max(s, axis=-1, keepdims=True))
    p = jnp.exp(s - m_new)
    alpha = jnp.exp(m - m_new)
    return (m_new, alpha * l + jnp.sum(p, axis=-1, keepdims=True),
            alpha * acc + _dot(p.astype(v.dtype), v))


def _online_init(m_sc, l_sc, acc_sc):
    m_sc[...] = jnp.full(m_sc.shape, -jnp.inf, F32)
    l_sc[...] = jnp.zeros(l_sc.shape, F32)
    acc_sc[...] = jnp.zeros(acc_sc.shape, F32)


def _pattn_kernel(qt_ref, k_ref, kvt_ref, o_ref, m_sc, l_sc, acc_sc, sa_sc, sb_sc):
    i = pl.program_id(1)
    heads, tq, _ = o_ref.shape
    cols = heads * tq
    _online_init(m_sc, l_sc, acc_sc)

    def keys_of(j):
        return pl.ds(pl.multiple_of(j * tq, tq), tq)

    def scores(j, st_ref):
        st_ref[...] = _dot(k_ref[keys_of(j), :], qt_ref[...])

    def update(j, st_ref, masked):
        st = st_ref[...]
        if masked:
            key = lax.broadcasted_iota(jnp.int32, st.shape, 0)
            query = lax.broadcasted_iota(jnp.int32, st.shape, 1) % tq
            st = jnp.where(key <= query, st, -jnp.inf)
        m_prev = m_sc[...]
        m_new = jnp.maximum(m_prev, jnp.max(st, axis=0, keepdims=True))
        p = jnp.exp(st - m_new)
        alpha = jnp.exp(m_prev - m_new)
        l_sc[...] = alpha * l_sc[...] + jnp.sum(p, axis=0, keepdims=True)
        acc_sc[...] = alpha * acc_sc[...] + _dot(kvt_ref[:, keys_of(j)], p.astype(BF16))
        m_sc[...] = m_new

    scores(0, sa_sc)

    def block_pair(pair, carry):
        j = 2 * pair
        scores(j + 1, sb_sc)
        update(j, sa_sc, False)
        scores(j + 2, sa_sc)
        update(j + 1, sb_sc, False)
        return carry

    lax.fori_loop(0, i // 2, block_pair, 0)

    @pl.when(i % 2 == 0)
    def _():
        update(i, sa_sc, True)

    @pl.when(i % 2 == 1)
    def _():
        scores(i, sb_sc)
        update(i - 1, sa_sc, False)
        update(i, sb_sc, True)

    ot = acc_sc[...] * (1.0 / l_sc[...])
    for hh in range(heads):
        o_ref[hh] = ot[:, hh * tq:(hh + 1) * tq].T.astype(BF16)


def _pattn(qt, kmat, kvt, *, batch, seq, tq):
    heads = qt.shape[1] // (batch * seq)
    nq = seq // tq
    return pl.pallas_call(
        _pattn_kernel, grid=(batch, nq),
        in_specs=[pl.BlockSpec((KW, heads * tq), lambda b, i: (0, b * nq + i)),
                  pl.BlockSpec((seq, KW), lambda b, i: (b, 0)),
                  pl.BlockSpec((KV_RANK, seq), lambda b, i: (0, b))],
        out_specs=pl.BlockSpec((heads, tq, KV_RANK), lambda b, i: (0, b * nq + i, 0)),
        out_shape=jax.ShapeDtypeStruct((heads, batch * seq, KV_RANK), BF16),
        scratch_shapes=[pltpu.VMEM((1, heads * tq), F32), pltpu.VMEM((1, heads * tq), F32),
                        pltpu.VMEM((KV_RANK, heads * tq), F32),
                        pltpu.VMEM((tq, heads * tq), F32), pltpu.VMEM((tq, heads * tq), F32)],
        compiler_params=_params(("arbitrary", "arbitrary")), name="pattn",
    )(qt, kmat, kvt)


def _ffn_dattn_kernel(pt_ref, x2_ref, hf_ref, wup_ref, wdn_ref, gfin_ref,
                      q_ref, kn_ref, ckv_hbm, cpe_hbm, y_ref, o_ref, *scratch,
                      layer, pages_per_chunk, nbuf, steps_per_batch, row_subtiles, new_tokens):
    assert nbuf == 4
    kvbufs, pebufs = scratch[0:nbuf], scratch[nbuf:2 * nbuf]
    sems, m_sc, l_sc, acc_sc = scratch[2 * nbuf:]
    f = pl.program_id(1)
    r = pl.program_id(2)
    nf = pl.num_programs(1)
    nr = row_subtiles
    n = (pl.program_id(0) * nf + f) * nr + r
    last_step = n + 1 == pl.num_programs(0) * nf * nr
    c = n % steps_per_batch
    sub_rows = y_ref.shape[0] // nr
    rows = pl.ds(pl.multiple_of(r * sub_rows, sub_rows), sub_rows)
    page = kvbufs[0].shape[0] // pages_per_chunk
    pages_per_step = nbuf * pages_per_chunk
    first_page = n * pages_per_step

    def start_chunk(first, buf):
        for j in range(pages_per_chunk):
            page_id = pt_ref[first + j]
            keys = slice(j * page, (j + 1) * page)
            pltpu.make_async_copy(ckv_hbm.at[layer, page_id], kvbufs[buf].at[keys], sems.at[0, buf]).start()
            pltpu.make_async_copy(cpe_hbm.at[layer, page_id], pebufs[buf].at[:, keys], sems.at[1, buf]).start()

    def wait_chunk(buf):
        pltpu.make_async_copy(kvbufs[buf], kvbufs[buf], sems.at[0, buf]).wait()
        pltpu.make_async_copy(pebufs[buf], pebufs[buf], sems.at[1, buf]).wait()

    @pl.when(n == 0)
    def _():
        _online_init(m_sc, l_sc, acc_sc)
        for buf in range(nbuf):
            start_chunk(first_page + buf * pages_per_chunk, buf)

    @pl.when(f == 0)
    def _():
        y_ref[rows, :] = x2_ref[rows, :]

    q = q_ref[0]
    qa = q[:, 0:KV_RANK].astype(F32)
    qp = q[:, KV_RANK:KV_RANK + QK_ROPE].astype(F32)

    kn = kn_ref[0]
    sn = _dot_nt(q, kn)
    t = lax.broadcasted_iota(jnp.int32, sn.shape, 0) % new_tokens
    cidx = lax.broadcasted_iota(jnp.int32, sn.shape, 1)
    sn = jnp.where(cidx <= t, sn, -jnp.inf)
    m_new = jnp.max(sn, axis=-1, keepdims=True)
    p_new = jnp.exp(sn - m_new)
    first = c == 0
    state = (jnp.where(first, m_new, m_sc[...]),
             jnp.where(first, jnp.sum(p_new, axis=-1, keepdims=True), l_sc[...]),
             jnp.where(first, _dot(p_new.astype(BF16), kn[:, 0:KV_RANK]), acc_sc[...]))

    qrows = q.shape[0]
    qa_pad = jnp.concatenate([qa, jnp.zeros((LANES - qrows, KV_RANK), F32)], axis=0)

    def scores(buf):
        st = _dot_nt(kvbufs[buf][...], qa_pad)
        return st.T[0:qrows] + _dot(qp, pebufs[buf][...])

    next_first = jnp.where(last_step, 0, first_page + pages_per_step)

    def attend(s, buf, state):
        state = _online_step(s, kvbufs[buf][...], *state)
        start_chunk(next_first + buf * pages_per_chunk, buf)
        return state

    half_d = wdn_ref.shape[1] // 2

    wait_chunk(0)
    wait_chunk(1)
    s0 = scores(0)
    s1 = scores(1)
    state = attend(s0, 0, state)
    a = jnp.maximum(_dot(hf_ref[rows, :], wup_ref[...]), 0.0)
    state = attend(s1, 1, state)
    a = (a * a).astype(BF16)

    wait_chunk(2)
    wait_chunk(3)
    s2 = scores(2)
    s3 = scores(3)
    state = attend(s2, 2, state)
    y_ref[rows, 0:half_d] += _dot(a, wdn_ref[:, 0:half_d])
    state = attend(s3, 3, state)
    y_ref[rows, half_d:] += _dot(a, wdn_ref[:, half_d:])
    m_sc[...], l_sc[...], acc_sc[...] = state

    @pl.when(last_step)
    def _():
        for buf in range(nbuf):
            wait_chunk(buf)

    @pl.when(c == steps_per_batch - 1)
    def _():
        o_ref[0] = (state[2] * (1.0 / state[1])).astype(BF16)

    @pl.when(f == nf - 1)
    def _():
        y_ref[rows, :] = _rms(y_ref[rows, :], gfin_ref[...])


def _ffn_dattn(x2, hf, wup, wdn, gfin, page_table, q, knew, cache_kv, cache_pe, *, tm, row_subtiles,
               tf, layer, pages_per_chunk, nbuf, new_tokens):
    rows, d = x2.shape
    d_ff = wup.shape[1]
    nbatch, qrows, _ = q.shape
    n_pages = page_table.shape[1]
    page = cache_kv.shape[2]
    keys = pages_per_chunk * page
    ni, nf, nr = rows // tm, d_ff // tf, row_subtiles
    steps_per_batch = n_pages // (nbuf * pages_per_chunk)
    assert ni * nf * nr == nbatch * steps_per_batch
    row_in_spec = pl.BlockSpec((tm, d), lambda i, f, r, pt: (i, 0), pipeline_mode=pl.Buffered(1))
    batch_spec = lambda nrow, width: pl.BlockSpec(
        (1, nrow, width), lambda i, f, r, pt: (((i * nf + f) * nr + r) // steps_per_batch, 0, 0))
    grid_spec = pltpu.PrefetchScalarGridSpec(
        num_scalar_prefetch=1, grid=(ni, nf, nr),
        in_specs=[row_in_spec, row_in_spec,
                  pl.BlockSpec((d, tf), lambda i, f, r, pt: (0, f)),
                  pl.BlockSpec((tf, d), lambda i, f, r, pt: (f, 0)), _const_spec((1, d)),
                  batch_spec(qrows, KW), batch_spec(LANES, KW),
                  pl.BlockSpec(memory_space=pl.ANY), pl.BlockSpec(memory_space=pl.ANY)],
        out_specs=[pl.BlockSpec((tm, d), lambda i, f, r, pt: (i, 0)), batch_spec(qrows, KV_RANK)],
        scratch_shapes=([pltpu.VMEM((keys, KV_RANK), F32)] * nbuf + [pltpu.VMEM((QK_ROPE, keys), F32)] * nbuf
                        + [pltpu.SemaphoreType.DMA((2, nbuf)),
                           pltpu.VMEM((qrows, 1), F32), pltpu.VMEM((qrows, 1), F32),
                           pltpu.VMEM((qrows, KV_RANK), F32)]))
    return pl.pallas_call(
        functools.partial(_ffn_dattn_kernel, layer=layer, pages_per_chunk=pages_per_chunk, nbuf=nbuf,
                          steps_per_batch=steps_per_batch, row_subtiles=nr, new_tokens=new_tokens),
        grid_spec=grid_spec,
        out_shape=[jax.ShapeDtypeStruct((rows, d), F32), jax.ShapeDtypeStruct((nbatch, qrows, KV_RANK), BF16)],
        compiler_params=_params(("arbitrary", "arbitrary", "arbitrary")), name="ffn_dattn",
    )(page_table.reshape(-1), x2, hf, wup, wdn, gfin, q, knew, cache_kv, cache_pe)


def _merge_kernel(x_ref, o_ref, cm_ref, wuv_ref, ag_ref, wout_ref, gmem_ref, wmq_ref, *rest, fuse_mem):
    if fuse_mem:
        mk_ref, mv_ref, wmo_ref, gffn_ref, xo_ref, ho_ref, attn_sc, om_sc = rest
    else:
        xo_ref, ho_ref, attn_sc = rest
    for hh in range(HEADS):
        attn_sc[:, hh * V_HEAD:(hh + 1) * V_HEAD] = _dot(o_ref[hh], wuv_ref[hh])
    attn_n = _rms(attn_sc[...], ag_ref[...]).astype(BF16)
    mix = _dot(attn_n, wout_ref[0:AW, :]) + _dot(cm_ref[...], wout_ref[AW:AW + CW, :])
    x1 = x_ref[...] + mix
    qm = _dot(_rms(x1, gmem_ref[...]).astype(BF16), wmq_ref[...]) * MEM_SCALE
    if not fuse_mem:
        xo_ref[...] = x1
        ho_ref[...] = qm.astype(BF16)
        return
    for hh in range(MEM_HEADS):
        cs = slice(hh * MEM_HD, (hh + 1) * MEM_HD)
        s = _dot_nt(qm[:, cs].astype(BF16), mk_ref[0, :, cs])
        p = jnp.exp(s - jnp.max(s, axis=-1, keepdims=True))
        oh = _dot(p.astype(BF16), mv_ref[0, :, cs])
        om_sc[:, cs] = (oh * (1.0 / jnp.sum(p, axis=-1, keepdims=True))).astype(BF16)
    x2 = x1 + _dot(om_sc[...], wmo_ref[...])
    xo_ref[...] = x2
    ho_ref[...] = _rms(x2, gffn_ref[...]).astype(BF16)


def _merge(x, o_lat, cm, w, mem_kv, *, tm, tiles_per_batch):
    rows, d = x.shape
    fuse_mem = mem_kv is not None
    row_spec = lambda width: pl.BlockSpec((tm, width), lambda i: (i, 0))
    in_specs = [row_spec(d), pl.BlockSpec((HEADS, tm, KV_RANK), lambda i: (0, i, 0)), row_spec(CW),
                _const_spec((HEADS, KV_RANK, V_HEAD)), _const_spec((1, AW)),
                _const_spec((AW + CW, d)), _const_spec((1, d)), _const_spec((d, MEM_INNER))]
    args = [x, o_lat, cm, w["wuv"], w["ag"], w["wout"], w["gmem"], w["wmq"]]
    scratch = [pltpu.VMEM((tm, AW), F32)]
    second_width = MEM_INNER
    if fuse_mem:
        mem_len = mem_kv[0].shape[1]
        mem_spec = pl.BlockSpec((1, mem_len, MEM_INNER), lambda i: (i // tiles_per_batch, 0, 0))
        in_specs += [mem_spec, mem_spec, _const_spec((MEM_INNER, d)), _const_spec((1, d))]
        args += list(mem_kv) + [w["wmo"], w["gffn"]]
        scratch.append(pltpu.VMEM((tm, MEM_INNER), BF16))
        second_width = d
    return pl.pallas_call(
        functools.partial(_merge_kernel, fuse_mem=fuse_mem), grid=(rows // tm,),
        in_specs=in_specs, out_specs=[row_spec(d), row_spec(second_width)],
        out_shape=[jax.ShapeDtypeStruct((rows, d), F32), jax.ShapeDtypeStruct((rows, second_width), BF16)],
        scratch_shapes=scratch,
        compiler_params=_params(("arbitrary",)), name="merge_mem" if fuse_mem else "merge",
    )(*args)


def _smem_kernel(q_ref, mk_ref, mv_ref, o_ref):
    rows, mrows = q_ref.shape[1], mk_ref.shape[1]
    qhead = lax.broadcasted_iota(jnp.int32, (rows, mrows), 0) % MEM_HEADS
    mhead = lax.broadcasted_iota(jnp.int32, (rows, mrows), 1) % MEM_HEADS
    same_head = qhead == mhead

    def body(bi, carry):
        s = _dot_nt(q_ref[bi], mk_ref[bi].astype(BF16))
        s = jnp.where(same_head, s, -jnp.inf)
        p = jnp.exp(s - jnp.max(s, axis=-1, keepdims=True))
        o = _dot(p.astype(BF16), mv_ref[bi].astype(BF16))
        o_ref[bi] = (o * (1.0 / jnp.sum(p, axis=-1, keepdims=True))).astype(BF16)
        return carry
    lax.fori_loop(0, q_ref.shape[0], body, 0)


def _smem(q, mk, mv, *, layer, bb):
    nbatch, rows, _ = q.shape
    mrows = mk.shape[1]
    nblk = nbatch // bb
    q_spec = pl.BlockSpec((bb, rows, MEM_HD), lambda i: (i, 0, 0))
    m_spec = pl.BlockSpec((bb, mrows, MEM_HD), lambda i: (layer * nblk + i, 0, 0))
    return pl.pallas_call(
        _smem_kernel, grid=(nblk,), in_specs=[q_spec, m_spec, m_spec], out_specs=q_spec,
        out_shape=jax.ShapeDtypeStruct(q.shape, BF16),
        compiler_params=_params(("arbitrary",)), name="smem",
    )(q, mk, mv)


def _ffn_kernel(x1_ref, om_ref, wmo_ref, gffn_ref, wup_ref, wdn_ref, gfin_ref, y_ref, hf_sc, acc_sc):
    f = pl.program_id(1)

    @pl.when(f == 0)
    def _():
        x2 = x1_ref[...] + _dot(om_ref[...], wmo_ref[...])
        acc_sc[...] = x2
        hf_sc[...] = _rms(x2, gffn_ref[...]).astype(BF16)

    a = jnp.maximum(_dot(hf_sc[...], wup_ref[...]), 0.0)
    acc_sc[...] += _dot((a * a).astype(BF16), wdn_ref[...])

    @pl.when(f == pl.num_programs(1) - 1)
    def _():
        y_ref[...] = _rms(acc_sc[...], gfin_ref[...])


def _ffn(x1, om, w, *, tm, tf):
    rows, d = x1.shape
    d_ff = w["wup"].shape[1]
    row_spec = lambda width: pl.BlockSpec((tm, width), lambda i, f: (i, 0))
    return pl.pallas_call(
        _ffn_kernel, grid=(rows // tm, d_ff // tf),
        in_specs=[row_spec(d), row_spec(MEM_INNER), _const_spec((MEM_INNER, d)), _const_spec((1, d)),
                  pl.BlockSpec((d, tf), lambda i, f: (0, f)), pl.BlockSpec((tf, d), lambda i, f: (f, 0)),
                  _const_spec((1, d))],
        out_specs=row_spec(d), out_shape=jax.ShapeDtypeStruct((rows, d), F32),
        scratch_shapes=[pltpu.VMEM((tm, d), BF16), pltpu.VMEM((tm, d), F32)],
        compiler_params=_params(("arbitrary", "arbitrary")), name="ffn",
    )(x1, om, w["wmo"], w["gffn"], w["wup"], w["wdn"], w["gfin"])


def _rope_tables(pos):
    half = QK_ROPE // 2
    inv = ROPE_THETA ** (-jnp.arange(half, dtype=F32) / half)
    ang = pos.astype(F32)[:, None] * inv[None, :]
    cos, sin, z = jnp.cos(ang), jnp.sin(ang), jnp.zeros_like(ang)
    return (jnp.concatenate([cos, cos, z, z], axis=1),
            jnp.concatenate([-sin, z, z, z], axis=1),
            jnp.concatenate([z, sin, z, z], axis=1))


def _layer_weights(l, norm_mix_g, w_in, q_norm_g, w_q_up, kv_norm_g, w_uk, w_uv, v_norm_g, attn_out_g,
                   cmlp_out_g, w_out, norm_mem_g, w_mem_q, w_mem_o, norm_ffn_g, w_ffn_up, w_ffn_down,
                   final_norm_g):
    d = w_in.shape[1]
    o_pe_end = Q_RANK + KV_RANK + QK_ROPE
    win = jnp.concatenate([w_in[l, :, :o_pe_end], jnp.zeros((d, LANES - QK_ROPE), F32),
                           w_in[l, :, o_pe_end:]], axis=1).astype(BF16)
    wq = w_q_up[l].reshape(Q_RANK, HEADS, QK_NOPE + QK_ROPE)
    wq = jnp.concatenate([wq, jnp.zeros((Q_RANK, HEADS, 2 * LANES - QK_NOPE - QK_ROPE), F32)], axis=2)
    row = lambda g: g.reshape(1, -1)
    return dict(
        gmix=row(norm_mix_g[l]), win=win, qg=row(q_norm_g[l]),
        wq=wq.reshape(Q_RANK, HEADS * 2 * LANES).astype(BF16), kvg=row(kv_norm_g[l]),
        wuk=jnp.transpose(w_uk[l], (1, 2, 0)).astype(BF16), vg=row(v_norm_g[l]),
        cmg=row(cmlp_out_g[l]), wuv=jnp.transpose(w_uv[l], (1, 0, 2)).astype(BF16),
        ag=row(attn_out_g[l]), wout=w_out[l].astype(BF16), gmem=row(norm_mem_g[l]),
        wmq=w_mem_q[l].astype(BF16), wmo=w_mem_o[l].astype(BF16), gffn=row(norm_ffn_g[l]),
        wup32=w_ffn_up[l], wdn32=w_ffn_down[l], gfin=row(final_norm_g))


def kernel(x_prompt, x_sample, cache_kv_latent, cache_k_rope, cache_mem_k, cache_mem_v, page_table,
           mem_prompt, norm_mix_g, w_in, q_norm_g, w_q_up, kv_norm_g, w_uk, w_uv, v_norm_g, w_spatial,
           b_spatial, attn_out_g, cmlp_out_g, w_out, norm_mem_g, mem_in_g, w_mem_q, w_mem_k, w_mem_v,
           w_mem_o, norm_ffn_g, w_ffn_up, w_ffn_down, final_norm_g):
    batch, seq, d = x_prompt.shape
    dbatch, t_new, _ = x_sample.shape
    depth = w_in.shape[0]
    assert depth == 1 and seq % CHUNK == 0 and CHUNK % t_new == 0
    l = 0
    mem_len = mem_prompt.shape[1]
    tm = 256
    srows = dbatch * t_new

    w = _layer_weights(l, norm_mix_g, w_in, q_norm_g, w_q_up, kv_norm_g, w_uk, w_uv, v_norm_g,
                       attn_out_g, cmlp_out_g, w_out, norm_mem_g, w_mem_q, w_mem_o, norm_ffn_g,
                       w_ffn_up, w_ffn_down, final_norm_g)

    xp = x_prompt.reshape(batch * seq, d)
    tabs_p = _rope_tables(jnp.arange(seq))
    qt_p, kmat_p, kvc_p, kpe_p, cm_p, kvt_p, w["wup"], w["wdn"] = _proj(
        xp, tabs_p, seq // tm, w_spatial[l], b_spatial[l].T, w, (w["wup32"], w["wdn32"]),
        tm=tm, prompt=True)
    mk_p, mv_p = _memkv(mem_prompt.reshape(batch * mem_len, d), mem_in_g[l].reshape(1, -1),
                        w_mem_k[l].astype(BF16), w_mem_v[l].astype(BF16), tm=tm)
    o_p = _pattn(qt_p, kmat_p, kvt_p, batch=batch, seq=seq, tq=tm)
    mem_kv_p = (mk_p.astype(BF16).reshape(batch, mem_len, MEM_INNER),
                mv_p.astype(BF16).reshape(batch, mem_len, MEM_INNER))
    x2_p, hf_p = _merge(xp, o_p, cm_p, w, mem_kv_p, tm=tm, tiles_per_batch=seq // tm)

    xs = x_sample.reshape(srows, d)
    reps = CHUNK // t_new
    tabs_s = _rope_tables(PAST_LEN + jnp.arange(tm) % t_new)
    eye = jnp.eye(reps, dtype=F32)
    wmix_s = jnp.einsum("ab,gts->gatbs", eye, w_spatial[l][:, :t_new, :t_new]).reshape(GROUPS, CHUNK, CHUNK)
    bmix_s = jnp.tile(b_spatial[l][:, :t_new].T, (reps, 1))
    q_s, kmat_s, kvc_s, kpe_s, cm_s, v_s = _proj(xs, tabs_s, 1, wmix_s, bmix_s, w, tm=tm, prompt=False)

    q_sb = jnp.transpose(q_s.reshape(HEADS, dbatch, t_new, KW), (1, 0, 2, 3)).reshape(dbatch, HEADS * t_new, KW)
    knew = jnp.pad(kmat_s.reshape(dbatch, t_new, KW), ((0, 0), (0, LANES - t_new), (0, 0)))
    y_p, o_sb = _ffn_dattn(x2_p, hf_p, w["wup"], w["wdn"], w["gfin"], page_table, q_sb, knew,
                           cache_kv_latent, jnp.swapaxes(cache_k_rope, 2, 3), tm=1024, row_subtiles=2,
                           tf=512, layer=l, pages_per_chunk=16, nbuf=4, new_tokens=t_new)
    o_s = jnp.transpose(o_sb.reshape(dbatch, HEADS, t_new, KV_RANK), (1, 0, 2, 3)).reshape(HEADS, srows, KV_RANK)
    x1_s, qm_s = _merge(xs, o_s, cm_s, w, None, tm=tm, tiles_per_batch=1)
    mem_rows = (depth * dbatch, mem_len * MEM_HEADS, MEM_HD)
    om_s = _smem(qm_s.reshape(dbatch, t_new * MEM_HEADS, MEM_HD), cache_mem_k.reshape(mem_rows),
                 cache_mem_v.reshape(mem_rows), layer=l, bb=8)
    y_s = _ffn(x1_s, om_s.reshape(srows, MEM_INNER), w, tm=512, tf=1024)

    mem_shape = (depth, batch, mem_len, MEM_HEADS, MEM_HD)
    return (y_p.reshape(batch, seq, d), y_s.reshape(dbatch, t_new, d),
            kvc_p.reshape(depth, batch, seq, KV_RANK), kpe_p.reshape(depth, batch, seq, QK_ROPE),
            mk_p.reshape(mem_shape), mv_p.reshape(mem_shape),
            kvc_s.reshape(depth, dbatch, t_new, KV_RANK), kpe_s.reshape(depth, dbatch, t_new, QK_ROPE),
            v_s.reshape(depth, dbatch, t_new, GROUPS, GW))
```

```python
import functools

import jax
import jax.numpy as jnp
from jax import lax
from jax.experimental import pallas as pl
from jax.experimental.pallas import tpu as pltpu

F32 = jnp.float32
BF16 = jnp.bfloat16

EPS = 1e-6
ROPE_THETA = 10000.0
PAST_LEN = 16384

LANES = 128
V7X_VMEM_BYTES = 64 * 2**20
VMEM_LIMIT = V7X_VMEM_BYTES - 8 * 2**20

Q_RANK = 512
KV_RANK = 256
QK_NOPE = 128
QK_ROPE = 64
HEADS = 8
V_HEAD = 128
GROUPS = 8
GW = 128
CHUNK = 128
CW = GROUPS * GW
AW = HEADS * V_HEAD
KW = KV_RANK + LANES
MEM_HEADS = 4
MEM_HD = 128
MEM_INNER = MEM_HEADS * MEM_HD
MLA_SCALE = (QK_NOPE + QK_ROPE) ** -0.5
MEM_SCALE = MEM_HD ** -0.5

O_Q = 0
O_KV = O_Q + Q_RANK
O_PE = O_KV + KV_RANK
O_U = O_PE + LANES
O_V = O_U + CW
D_IN_PAD = O_V + CW

NT_DIMS = (((1,), (1,)), ((), ()))


def _const_spec(shape):
    return pl.BlockSpec(shape, lambda *_: (0,) * len(shape), pipeline_mode=pl.Buffered(1))


def _params(semantics):
    return pltpu.CompilerParams(dimension_semantics=semantics, vmem_limit_bytes=VMEM_LIMIT)


def _rms(x, g):
    return x * lax.rsqrt(jnp.mean(x * x, axis=-1, keepdims=True) + EPS) * g


def _gelu(x):
    return 0.5 * x * (1.0 + jnp.tanh(0.7978845608028654 * (x + 0.044715 * (x * x * x))))


def _dot(a, b):
    return jnp.dot(a, b, preferred_element_type=F32)


def _dot_nt(a, b):
    return lax.dot_general(a, b, NT_DIMS, preferred_element_type=F32)


def _proj_kernel(x_ref, gmix_ref, win_ref, qg_ref, wq_ref, kvg_ref, wuk_ref, vg_ref,
                 cos_ref, s1_ref, s2_ref, wmix_ref, bmix_ref, cmg_ref, *rest, prompt):
    if prompt:
        wup32_ref, wdn32_ref, q_ref, kmat_ref, kvc_ref, kpe_ref, cm_ref, extra_ref, wup_ref, wdn_ref, cm_sc = rest
        wup_ref[...] = wup32_ref[...].astype(BF16)
        wdn_ref[...] = wdn32_ref[...].astype(BF16)
    else:
        q_ref, kmat_ref, kvc_ref, kpe_ref, cm_ref, extra_ref, cm_sc = rest
    tm = x_ref.shape[0]
    h = _rms(x_ref[...], gmix_ref[...]).astype(BF16)

    def seg(lo, hi):
        return _dot(h, win_ref[:, lo:hi])

    cosv, s1, s2 = cos_ref[...], s1_ref[...], s2_ref[...]

    def rope(p):
        return (p * cosv + pltpu.roll(p, LANES - QK_ROPE // 2, 1) * s1
                + pltpu.roll(p, QK_ROPE // 2, 1) * s2)

    qc = _rms(seg(O_Q, O_KV), qg_ref[...]).astype(BF16)
    q = _dot(qc, wq_ref[...])
    for hh in range(HEADS):
        base = hh * 2 * LANES
        qa = _dot(q[:, base:base + QK_NOPE].astype(BF16), wuk_ref[hh]) * MLA_SCALE
        qp = rope(q[:, base + LANES:base + 2 * LANES]) * MLA_SCALE
        if prompt:
            cols = slice(hh * tm, (hh + 1) * tm)
            q_ref[0:KV_RANK, cols] = qa.T.astype(BF16)
            q_ref[KV_RANK:KW, cols] = qp.T.astype(BF16)
        else:
            q_ref[hh, :, 0:KV_RANK] = qa.astype(BF16)
            q_ref[hh, :, KV_RANK:KW] = qp.astype(BF16)

    kv = _rms(seg(O_KV, O_PE), kvg_ref[...])
    kvc_ref[...] = kv
    kmat_ref[:, 0:KV_RANK] = kv.astype(BF16)
    if prompt:
        extra_ref[...] = kv.T.astype(BF16)
    kp = rope(seg(O_PE, O_U))
    kpe_ref[...] = kp[:, 0:QK_ROPE]
    kmat_ref[:, KV_RANK:KW] = kp.astype(BF16)

    u = _gelu(seg(O_U, O_V))
    v = _rms(_gelu(seg(O_V, D_IN_PAD)), vg_ref[...])
    if not prompt:
        extra_ref[...] = v
    vb = v.astype(BF16)
    row = lax.broadcasted_iota(jnp.int32, (CHUNK, CHUNK), 0)
    col = lax.broadcasted_iota(jnp.int32, (CHUNK, CHUNK), 1)
    for g in range(GROUPS):
        wg = jnp.where(row >= col, wmix_ref[g], 0.0).astype(BF16)
        bg = bmix_ref[:, g:g + 1]
        for c in range(tm // CHUNK):
            rs = slice(c * CHUNK, (c + 1) * CHUNK)
            cs = slice(g * GW, (g + 1) * GW)
            cm_sc[rs, cs] = u[rs, cs] * (_dot(wg, vb[rs, cs]) + bg)
    cm_ref[...] = _rms(cm_sc[...], cmg_ref[...]).astype(BF16)


def _proj(x, tabs, tab_blocks, wmix, bmix, w, ffn_weights=None, *, tm, prompt):
    rows, d = x.shape
    n = rows // tm
    cos_t, s1_t, s2_t = tabs
    tab_spec = pl.BlockSpec((tm, LANES), lambda i: (i % tab_blocks, 0))
    row_spec = lambda width: pl.BlockSpec((tm, width), lambda i: (i, 0))
    in_specs = [
        row_spec(d), _const_spec((1, d)), _const_spec((d, D_IN_PAD)), _const_spec((1, Q_RANK)),
        _const_spec((Q_RANK, HEADS * 2 * LANES)), _const_spec((1, KV_RANK)),
        _const_spec((HEADS, QK_NOPE, KV_RANK)), _const_spec((1, CW)),
        tab_spec, tab_spec, tab_spec,
        _const_spec((GROUPS, CHUNK, CHUNK)), _const_spec((CHUNK, GROUPS)), _const_spec((1, CW)),
    ]
    if prompt:
        q_shape, q_spec = (KW, HEADS * rows), pl.BlockSpec((KW, HEADS * tm), lambda i: (0, i))
        extra_shape = jax.ShapeDtypeStruct((KV_RANK, rows), BF16)
        extra_spec = pl.BlockSpec((KV_RANK, tm), lambda i: (0, i))
    else:
        q_shape, q_spec = (HEADS, rows, KW), pl.BlockSpec((HEADS, tm, KW), lambda i: (0, i, 0))
        extra_shape, extra_spec = jax.ShapeDtypeStruct((rows, CW), F32), row_spec(CW)
    out_shape = [
        jax.ShapeDtypeStruct(q_shape, BF16),
        jax.ShapeDtypeStruct((rows, KW), BF16),
        jax.ShapeDtypeStruct((rows, KV_RANK), F32),
        jax.ShapeDtypeStruct((rows, QK_ROPE), F32),
        jax.ShapeDtypeStruct((rows, CW), BF16),
        extra_shape,
    ]
    out_specs = [q_spec, row_spec(KW), row_spec(KV_RANK), row_spec(QK_ROPE), row_spec(CW), extra_spec]
    args = [x, w["gmix"], w["win"], w["qg"], w["wq"], w["kvg"], w["wuk"], w["vg"],
            cos_t, s1_t, s2_t, wmix, bmix, w["cmg"]]
    if prompt:
        wup32, wdn32 = ffn_weights
        slab = wup32.shape[1] // n
        up_spec = pl.BlockSpec((d, slab), lambda i: (0, i))
        dn_spec = pl.BlockSpec((slab, d), lambda i: (i, 0))
        in_specs += [up_spec, dn_spec]
        args += [wup32, wdn32]
        out_shape += [jax.ShapeDtypeStruct(wup32.shape, BF16), jax.ShapeDtypeStruct(wdn32.shape, BF16)]
        out_specs += [up_spec, dn_spec]
    return pl.pallas_call(
        functools.partial(_proj_kernel, prompt=prompt),
        grid=(n,), in_specs=in_specs, out_specs=out_specs, out_shape=out_shape,
        scratch_shapes=[pltpu.VMEM((tm, CW), F32)],
        compiler_params=_params(("arbitrary",)),
        name="proj_prompt" if prompt else "proj_sample",
    )(*args)


def _memkv_kernel(m_ref, g_ref, wk_ref, wv_ref, k_ref, v_ref):
    m = _rms(m_ref[...], g_ref[...]).astype(BF16)
    k_ref[...] = _dot(m, wk_ref[...])
    v_ref[...] = _dot(m, wv_ref[...])


def _memkv(mem, g, wk, wv, *, tm):
    rows, d = mem.shape
    row_spec = lambda width: pl.BlockSpec((tm, width), lambda i: (i, 0))
    return pl.pallas_call(
        _memkv_kernel, grid=(rows // tm,),
        in_specs=[row_spec(d), _const_spec((1, d)), _const_spec((d, MEM_INNER)),
                  _const_spec((d, MEM_INNER))],
        out_specs=[row_spec(MEM_INNER), row_spec(MEM_INNER)],
        out_shape=[jax.ShapeDtypeStruct((rows, MEM_INNER), F32)] * 2,
        compiler_params=_params(("arbitrary",)), name="memkv",
    )(mem, g, wk, wv)


def _online_step(s, v, m, l, acc):
    m_new = jnp.maximum(m, jnp.max(s, axis=-1, keepdims=True))
    p = jnp.exp(s - m_new)
    alpha = jnp.exp(m - m_new)
    return (m_new, alpha * l + jnp.sum(p, axis=-1, keepdims=True),
            alpha * acc + _dot(p.astype(v.dtype), v))


def _online_init(m_sc, l_sc, acc_sc):
    m_sc[...] = jnp.full(m_sc.shape, -jnp.inf, F32)
    l_sc[...] = jnp.zeros(l_sc.shape, F32)
    acc_sc[...] = jnp.zeros(acc_sc.shape, F32)


def _pattn_kernel(qt_ref, k_ref, kvt_ref, o_ref, m_sc, l_sc, acc_sc, sa_sc, sb_sc):
    i = pl.program_id(1)
    heads, tq, _ = o_ref.shape
    cols = heads * tq
    _online_init(m_sc, l_sc, acc_sc)

    def keys_of(j):
        return pl.ds(pl.multiple_of(j * tq, tq), tq)

    def scores(j, st_ref):
        st_ref[...] = _dot(k_ref[keys_of(j), :], qt_ref[...])

    def update(j, st_ref, masked):
        st = st_ref[...]
        if masked:
            key = lax.broadcasted_iota(jnp.int32, st.shape, 0)
            query = lax.broadcasted_iota(jnp.int32, st.shape, 1) % tq
            st = jnp.where(key <= query, st, -jnp.inf)
        m_prev = m_sc[...]
        m_new = jnp.maximum(m_prev, jnp.max(st, axis=0, keepdims=True))
        p = jnp.exp(st - m_new)
        alpha = jnp.exp(m_prev - m_new)
        l_sc[...] = alpha * l_sc[...] + jnp.sum(p, axis=0, keepdims=True)
        acc_sc[...] = alpha * acc_sc[...] + _dot(kvt_ref[:, keys_of(j)], p.astype(BF16))
        m_sc[...] = m_new

    scores(0, sa_sc)

    def block_pair(pair, carry):
        j = 2 * pair
        scores(j + 1, sb_sc)
        update(j, sa_sc, False)
        scores(j + 2, sa_sc)
        update(j + 1, sb_sc, False)
        return carry

    lax.fori_loop(0, i // 2, block_pair, 0)

    @pl.when(i % 2 == 0)
    def _():
        update(i, sa_sc, True)

    @pl.when(i % 2 == 1)
    def _():
        scores(i, sb_sc)
        update(i - 1, sa_sc, False)
        update(i, sb_sc, True)

    ot = acc_sc[...] * (1.0 / l_sc[...])
    for hh in range(heads):
        o_ref[hh] = ot[:, hh * tq:(hh + 1) * tq].T.astype(BF16)


def _pattn(qt, kmat, kvt, *, batch, seq, tq):
    heads = qt.shape[1] // (batch * seq)
    nq = seq // tq
    return pl.pallas_call(
        _pattn_kernel, grid=(batch, nq),
        in_specs=[pl.BlockSpec((KW, heads * tq), lambda b, i: (0, b * nq + i)),
                  pl.BlockSpec((seq, KW), lambda b, i: (b, 0)),
                  pl.BlockSpec((KV_RANK, seq), lambda b, i: (0, b))],
        out_specs=pl.BlockSpec((heads, tq, KV_RANK), lambda b, i: (0, b * nq + i, 0)),
        out_shape=jax.ShapeDtypeStruct((heads, batch * seq, KV_RANK), BF16),
        scratch_shapes=[pltpu.VMEM((1, heads * tq), F32), pltpu.VMEM((1, heads * tq), F32),
                        pltpu.VMEM((KV_RANK, heads * tq), F32),
                        pltpu.VMEM((tq, heads * tq), F32), pltpu.VMEM((tq, heads * tq), F32)],
        compiler_params=_params(("arbitrary", "arbitrary")), name="pattn",
    )(qt, kmat, kvt)


def _ffn_dattn_kernel(pt_ref, x2_ref, hf_ref, wup_ref, wdn_ref, gfin_ref,
                      q_ref, kn_ref, ckv_hbm, cpe_hbm, y_ref, o_ref, facc_sc, *scratch,
                      layer, pages_per_chunk, nbuf, steps_per_batch, new_tokens):
    assert nbuf == 4
    kvbufs, pebufs = scratch[0:nbuf], scratch[nbuf:2 * nbuf]
    sems, m_sc, l_sc, acc_sc = scratch[2 * nbuf:]
    f = pl.program_id(1)
    nf = pl.num_programs(1)
    n = pl.program_id(0) * nf + f
    last_step = n + 1 == pl.num_programs(0) * nf
    c = n % steps_per_batch
    page = kvbufs[0].shape[0] // pages_per_chunk
    pages_per_round = nbuf * pages_per_chunk
    first_page = n * 2 * pages_per_round

    def start_chunk(first, buf):
        for j in range(pages_per_chunk):
            page_id = pt_ref[first + j]
            keys = slice(j * page, (j + 1) * page)
            pltpu.make_async_copy(ckv_hbm.at[layer, page_id], kvbufs[buf].at[keys], sems.at[0, buf]).start()
            pltpu.make_async_copy(cpe_hbm.at[layer, page_id], pebufs[buf].at[:, keys], sems.at[1, buf]).start()

    def wait_chunk(buf):
        pltpu.make_async_copy(kvbufs[buf], kvbufs[buf], sems.at[0, buf]).wait()
        pltpu.make_async_copy(pebufs[buf], pebufs[buf], sems.at[1, buf]).wait()

    @pl.when(n == 0)
    def _():
        _online_init(m_sc, l_sc, acc_sc)
        for buf in range(nbuf):
            start_chunk(first_page + buf * pages_per_chunk, buf)

    @pl.when(f == 0)
    def _():
        facc_sc[...] = x2_ref[...]

    q = q_ref[0]
    qa = q[:, 0:KV_RANK].astype(F32)
    qp = q[:, KV_RANK:KV_RANK + QK_ROPE].astype(F32)

    kn = kn_ref[0]
    sn = _dot_nt(q, kn)
    t = lax.broadcasted_iota(jnp.int32, sn.shape, 0) % new_tokens
    cidx = lax.broadcasted_iota(jnp.int32, sn.shape, 1)
    sn = jnp.where(cidx <= t, sn, -jnp.inf)
    m_new = jnp.max(sn, axis=-1, keepdims=True)
    p_new = jnp.exp(sn - m_new)
    first = c == 0
    state = (jnp.where(first, m_new, m_sc[...]),
             jnp.where(first, jnp.sum(p_new, axis=-1, keepdims=True), l_sc[...]),
             jnp.where(first, _dot(p_new.astype(BF16), kn[:, 0:KV_RANK]), acc_sc[...]))

    qrows = q.shape[0]
    qa_pad = jnp.concatenate([qa, jnp.zeros((LANES - qrows, KV_RANK), F32)], axis=0)

    def scores(buf):
        st = _dot_nt(kvbufs[buf][...], qa_pad)
        return st.T[0:qrows] + _dot(qp, pebufs[buf][...])

    half_f = wup_ref.shape[1] // 2
    half_d = wdn_ref.shape[1] // 2
    ffn = {}

    def relu2(u):
        u = jnp.maximum(u, 0.0)
        return (u * u).astype(BF16)

    def ffn_piece(k):
        if k == 0:
            ffn["lo"] = relu2(_dot(hf_ref[...], wup_ref[:, 0:half_f]))
        elif k == 1:
            ffn["a"] = jnp.concatenate([ffn["lo"], relu2(_dot(hf_ref[...], wup_ref[:, half_f:]))], axis=1)
        elif k == 2:
            facc_sc[:, 0:half_d] += _dot(ffn["a"], wdn_ref[:, 0:half_d])
        else:
            facc_sc[:, half_d:] += _dot(ffn["a"], wdn_ref[:, half_d:])

    for rnd in range(2):
        nxt = first_page + (rnd + 1) * pages_per_round
        if rnd == 1:
            nxt = jnp.where(last_step, 0, nxt)

        def attend(s, buf, state, nxt=nxt):
            state = _online_step(s, kvbufs[buf][...], *state)
            start_chunk(nxt + buf * pages_per_chunk, buf)
            return state

        for pair in range(2):
            b0, b1 = 2 * pair, 2 * pair + 1
            wait_chunk(b0)
            wait_chunk(b1)
            s0 = scores(b0)
            s1 = scores(b1)
            state = attend(s0, b0, state)
            ffn_piece(2 * rnd + pair)
            state = attend(s1, b1, state)
    m_sc[...], l_sc[...], acc_sc[...] = state

    @pl.when(last_step)
    def _():
        for buf in range(nbuf):
            wait_chunk(buf)

    @pl.when(c == steps_per_batch - 1)
    def _():
        o_ref[0] = (state[2] * (1.0 / state[1])).astype(BF16)

    @pl.when(f == nf - 1)
    def _():
        y_ref[...] = _rms(facc_sc[...], gfin_ref[...])


def _ffn_dattn(x2, hf, wup, wdn, gfin, page_table, q, knew, cache_kv, cache_pe, *, tm, tf, layer,
               pages_per_chunk, nbuf, new_tokens):
    rows, d = x2.shape
    d_ff = wup.shape[1]
    nbatch, qrows, _ = q.shape
    n_pages = page_table.shape[1]
    page = cache_kv.shape[2]
    keys = pages_per_chunk * page
    ni, nf = rows // tm, d_ff // tf
    steps_per_batch = n_pages // (2 * nbuf * pages_per_chunk)
    assert ni * nf == nbatch * steps_per_batch
    row_spec = lambda width: pl.BlockSpec((tm, width), lambda i, f, pt: (i, 0))
    batch_spec = lambda r, width: pl.BlockSpec(
        (1, r, width), lambda i, f, pt: ((i * nf + f) // steps_per_batch, 0, 0))
    grid_spec = pltpu.PrefetchScalarGridSpec(
        num_scalar_prefetch=1, grid=(ni, nf),
        in_specs=[row_spec(d), row_spec(d),
                  pl.BlockSpec((d, tf), lambda i, f, pt: (0, f)),
                  pl.BlockSpec((tf, d), lambda i, f, pt: (f, 0)), _const_spec((1, d)),
                  batch_spec(qrows, KW), batch_spec(LANES, KW),
                  pl.BlockSpec(memory_space=pl.ANY), pl.BlockSpec(memory_space=pl.ANY)],
        out_specs=[row_spec(d), batch_spec(qrows, KV_RANK)],
        scratch_shapes=([pltpu.VMEM((tm, d), F32)]
                        + [pltpu.VMEM((keys, KV_RANK), F32)] * nbuf + [pltpu.VMEM((QK_ROPE, keys), F32)] * nbuf
                        + [pltpu.SemaphoreType.DMA((2, nbuf)),
                           pltpu.VMEM((qrows, 1), F32), pltpu.VMEM((qrows, 1), F32),
                           pltpu.VMEM((qrows, KV_RANK), F32)]))
    return pl.pallas_call(
        functools.partial(_ffn_dattn_kernel, layer=layer, pages_per_chunk=pages_per_chunk, nbuf=nbuf,
                          steps_per_batch=steps_per_batch, new_tokens=new_tokens),
        grid_spec=grid_spec,
        out_shape=[jax.ShapeDtypeStruct((rows, d), F32), jax.ShapeDtypeStruct((nbatch, qrows, KV_RANK), BF16)],
        compiler_params=_params(("arbitrary", "arbitrary")), name="ffn_dattn",
    )(page_table.reshape(-1), x2, hf, wup, wdn, gfin, q, knew, cache_kv, cache_pe)


def _merge_kernel(x_ref, o_ref, cm_ref, wuv_ref, ag_ref, wout_ref, gmem_ref, wmq_ref, *rest, fuse_mem):
    if fuse_mem:
        mk_ref, mv_ref, wmo_ref, gffn_ref, xo_ref, ho_ref, attn_sc, om_sc = rest
    else:
        xo_ref, ho_ref, attn_sc = rest
    for hh in range(HEADS):
        attn_sc[:, hh * V_HEAD:(hh + 1) * V_HEAD] = _dot(o_ref[hh], wuv_ref[hh])
    attn_n = _rms(attn_sc[...], ag_ref[...]).astype(BF16)
    mix = _dot(attn_n, wout_ref[0:AW, :]) + _dot(cm_ref[...], wout_ref[AW:AW + CW, :])
    x1 = x_ref[...] + mix
    qm = _dot(_rms(x1, gmem_ref[...]).astype(BF16), wmq_ref[...]) * MEM_SCALE
    if not fuse_mem:
        xo_ref[...] = x1
        ho_ref[...] = qm.astype(BF16)
        return
    for hh in range(MEM_HEADS):
        cs = slice(hh * MEM_HD, (hh + 1) * MEM_HD)
        s = _dot_nt(qm[:, cs].astype(BF16), mk_ref[0, :, cs])
        p = jnp.exp(s - jnp.max(s, axis=-1, keepdims=True))
        oh = _dot(p.astype(BF16), mv_ref[0, :, cs])
        om_sc[:, cs] = (oh * (1.0 / jnp.sum(p, axis=-1, keepdims=True))).astype(BF16)
    x2 = x1 + _dot(om_sc[...], wmo_ref[...])
    xo_ref[...] = x2
    ho_ref[...] = _rms(x2, gffn_ref[...]).astype(BF16)


def _merge(x, o_lat, cm, w, mem_kv, *, tm, tiles_per_batch):
    rows, d = x.shape
    fuse_mem = mem_kv is not None
    row_spec = lambda width: pl.BlockSpec((tm, width), lambda i: (i, 0))
    in_specs = [row_spec(d), pl.BlockSpec((HEADS, tm, KV_RANK), lambda i: (0, i, 0)), row_spec(CW),
                _const_spec((HEADS, KV_RANK, V_HEAD)), _const_spec((1, AW)),
                _const_spec((AW + CW, d)), _const_spec((1, d)), _const_spec((d, MEM_INNER))]
    args = [x, o_lat, cm, w["wuv"], w["ag"], w["wout"], w["gmem"], w["wmq"]]
    scratch = [pltpu.VMEM((tm, AW), F32)]
    second_width = MEM_INNER
    if fuse_mem:
        mem_len = mem_kv[0].shape[1]
        mem_spec = pl.BlockSpec((1, mem_len, MEM_INNER), lambda i: (i // tiles_per_batch, 0, 0))
        in_specs += [mem_spec, mem_spec, _const_spec((MEM_INNER, d)), _const_spec((1, d))]
        args += list(mem_kv) + [w["wmo"], w["gffn"]]
        scratch.append(pltpu.VMEM((tm, MEM_INNER), BF16))
        second_width = d
    return pl.pallas_call(
        functools.partial(_merge_kernel, fuse_mem=fuse_mem), grid=(rows // tm,),
        in_specs=in_specs, out_specs=[row_spec(d), row_spec(second_width)],
        out_shape=[jax.ShapeDtypeStruct((rows, d), F32), jax.ShapeDtypeStruct((rows, second_width), BF16)],
        scratch_shapes=scratch,
        compiler_params=_params(("arbitrary",)), name="merge_mem" if fuse_mem else "merge",
    )(*args)


def _smem_kernel(q_ref, mk_ref, mv_ref, o_ref):
    rows, mrows = q_ref.shape[1], mk_ref.shape[1]
    qhead = lax.broadcasted_iota(jnp.int32, (rows, mrows), 0) % MEM_HEADS
    mhead = lax.broadcasted_iota(jnp.int32, (rows, mrows), 1) % MEM_HEADS
    same_head = qhead == mhead

    def body(bi, carry):
        s = _dot_nt(q_ref[bi], mk_ref[bi].astype(BF16))
        s = jnp.where(same_head, s, -jnp.inf)
        p = jnp.exp(s - jnp.max(s, axis=-1, keepdims=True))
        o = _dot(p.astype(BF16), mv_ref[bi].astype(BF16))
        o_ref[bi] = (o * (1.0 / jnp.sum(p, axis=-1, keepdims=True))).astype(BF16)
        return carry
    lax.fori_loop(0, q_ref.shape[0], body, 0)


def _smem(q, mk, mv, *, layer, bb):
    nbatch, rows, _ = q.shape
    mrows = mk.shape[1]
    nblk = nbatch // bb
    q_spec = pl.BlockSpec((bb, rows, MEM_HD), lambda i: (i, 0, 0))
    m_spec = pl.BlockSpec((bb, mrows, MEM_HD), lambda i: (layer * nblk + i, 0, 0))
    return pl.pallas_call(
        _smem_kernel, grid=(nblk,), in_specs=[q_spec, m_spec, m_spec], out_specs=q_spec,
        out_shape=jax.ShapeDtypeStruct(q.shape, BF16),
        compiler_params=_params(("arbitrary",)), name="smem",
    )(q, mk, mv)


def _ffn_kernel(x1_ref, om_ref, wmo_ref, gffn_ref, wup_ref, wdn_ref, gfin_ref, y_ref, hf_sc, acc_sc):
    f = pl.program_id(1)

    @pl.when(f == 0)
    def _():
        x2 = x1_ref[...] + _dot(om_ref[...], wmo_ref[...])
        acc_sc[...] = x2
        hf_sc[...] = _rms(x2, gffn_ref[...]).astype(BF16)

    a = jnp.maximum(_dot(hf_sc[...], wup_ref[...]), 0.0)
    acc_sc[...] += _dot((a * a).astype(BF16), wdn_ref[...])

    @pl.when(f == pl.num_programs(1) - 1)
    def _():
        y_ref[...] = _rms(acc_sc[...], gfin_ref[...])


def _ffn(x1, om, w, *, tm, tf):
    rows, d = x1.shape
    d_ff = w["wup"].shape[1]
    row_spec = lambda width: pl.BlockSpec((tm, width), lambda i, f: (i, 0))
    return pl.pallas_call(
        _ffn_kernel, grid=(rows // tm, d_ff // tf),
        in_specs=[row_spec(d), row_spec(MEM_INNER), _const_spec((MEM_INNER, d)), _const_spec((1, d)),
                  pl.BlockSpec((d, tf), lambda i, f: (0, f)), pl.BlockSpec((tf, d), lambda i, f: (f, 0)),
                  _const_spec((1, d))],
        out_specs=row_spec(d), out_shape=jax.ShapeDtypeStruct((rows, d), F32),
        scratch_shapes=[pltpu.VMEM((tm, d), BF16), pltpu.VMEM((tm, d), F32)],
        compiler_params=_params(("arbitrary", "arbitrary")), name="ffn",
    )(x1, om, w["wmo"], w["gffn"], w["wup"], w["wdn"], w["gfin"])


def _rope_tables(pos):
    half = QK_ROPE // 2
    inv = ROPE_THETA ** (-jnp.arange(half, dtype=F32) / half)
    ang = pos.astype(F32)[:, None] * inv[None, :]
    cos, sin, z = jnp.cos(ang), jnp.sin(ang), jnp.zeros_like(ang)
    return (jnp.concatenate([cos, cos, z, z], axis=1),
            jnp.concatenate([-sin, z, z, z], axis=1),
            jnp.concatenate([z, sin, z, z], axis=1))


def _layer_weights(l, norm_mix_g, w_in, q_norm_g, w_q_up, kv_norm_g, w_uk, w_uv, v_norm_g, attn_out_g,
                   cmlp_out_g, w_out, norm_mem_g, w_mem_q, w_mem_o, norm_ffn_g, w_ffn_up, w_ffn_down,
                   final_norm_g):
    d = w_in.shape[1]
    o_pe_end = Q_RANK + KV_RANK + QK_ROPE
    win = jnp.concatenate([w_in[l, :, :o_pe_end], jnp.zeros((d, LANES - QK_ROPE), F32),
                           w_in[l, :, o_pe_end:]], axis=1).astype(BF16)
    wq = w_q_up[l].reshape(Q_RANK, HEADS, QK_NOPE + QK_ROPE)
    wq = jnp.concatenate([wq, jnp.zeros((Q_RANK, HEADS, 2 * LANES - QK_NOPE - QK_ROPE), F32)], axis=2)
    row = lambda g: g.reshape(1, -1)
    return dict(
        gmix=row(norm_mix_g[l]), win=win, qg=row(q_norm_g[l]),
        wq=wq.reshape(Q_RANK, HEADS * 2 * LANES).astype(BF16), kvg=row(kv_norm_g[l]),
        wuk=jnp.transpose(w_uk[l], (1, 2, 0)).astype(BF16), vg=row(v_norm_g[l]),
        cmg=row(cmlp_out_g[l]), wuv=jnp.transpose(w_uv[l], (1, 0, 2)).astype(BF16),
        ag=row(attn_out_g[l]), wout=w_out[l].astype(BF16), gmem=row(norm_mem_g[l]),
        wmq=w_mem_q[l].astype(BF16), wmo=w_mem_o[l].astype(BF16), gffn=row(norm_ffn_g[l]),
        wup32=w_ffn_up[l], wdn32=w_ffn_down[l], gfin=row(final_norm_g))


def kernel(x_prompt, x_sample, cache_kv_latent, cache_k_rope, cache_mem_k, cache_mem_v, page_table,
           mem_prompt, norm_mix_g, w_in, q_norm_g, w_q_up, kv_norm_g, w_uk, w_uv, v_norm_g, w_spatial,
           b_spatial, attn_out_g, cmlp_out_g, w_out, norm_mem_g, mem_in_g, w_mem_q, w_mem_k, w_mem_v,
           w_mem_o, norm_ffn_g, w_ffn_up, w_ffn_down, final_norm_g):
    batch, seq, d = x_prompt.shape
    dbatch, t_new, _ = x_sample.shape
    depth = w_in.shape[0]
    assert depth == 1 and seq % CHUNK == 0 and CHUNK % t_new == 0
    l = 0
    mem_len = mem_prompt.shape[1]
    tm = 256
    srows = dbatch * t_new

    w = _layer_weights(l, norm_mix_g, w_in, q_norm_g, w_q_up, kv_norm_g, w_uk, w_uv, v_norm_g,
                       attn_out_g, cmlp_out_g, w_out, norm_mem_g, w_mem_q, w_mem_o, norm_ffn_g,
                       w_ffn_up, w_ffn_down, final_norm_g)

    xp = x_prompt.reshape(batch * seq, d)
    tabs_p = _rope_tables(jnp.arange(seq))
    qt_p, kmat_p, kvc_p, kpe_p, cm_p, kvt_p, w["wup"], w["wdn"] = _proj(
        xp, tabs_p, seq // tm, w_spatial[l], b_spatial[l].T, w, (w["wup32"], w["wdn32"]),
        tm=tm, prompt=True)
    mk_p, mv_p = _memkv(mem_prompt.reshape(batch * mem_len, d), mem_in_g[l].reshape(1, -1),
                        w_mem_k[l].astype(BF16), w_mem_v[l].astype(BF16), tm=tm)
    o_p = _pattn(qt_p, kmat_p, kvt_p, batch=batch, seq=seq, tq=tm)
    mem_kv_p = (mk_p.astype(BF16).reshape(batch, mem_len, MEM_INNER),
                mv_p.astype(BF16).reshape(batch, mem_len, MEM_INNER))
    x2_p, hf_p = _merge(xp, o_p, cm_p, w, mem_kv_p, tm=tm, tiles_per_batch=seq // tm)

    xs = x_sample.reshape(srows, d)
    reps = CHUNK // t_new
    tabs_s = _rope_tables(PAST_LEN + jnp.arange(tm) % t_new)
    eye = jnp.eye(reps, dtype=F32)
    wmix_s = jnp.einsum("ab,gts->gatbs", eye, w_spatial[l][:, :t_new, :t_new]).reshape(GROUPS, CHUNK, CHUNK)
    bmix_s = jnp.tile(b_spatial[l][:, :t_new].T, (reps, 1))
    q_s, kmat_s, kvc_s, kpe_s, cm_s, v_s = _proj(xs, tabs_s, 1, wmix_s, bmix_s, w, tm=tm, prompt=False)

    q_sb = jnp.transpose(q_s.reshape(HEADS, dbatch, t_new, KW), (1, 0, 2, 3)).reshape(dbatch, HEADS * t_new, KW)
    knew = jnp.pad(kmat_s.reshape(dbatch, t_new, KW), ((0, 0), (0, LANES - t_new), (0, 0)))
    y_p, o_sb = _ffn_dattn(x2_p, hf_p, w["wup"], w["wdn"], w["gfin"], page_table, q_sb, knew,
                           cache_kv_latent, jnp.swapaxes(cache_k_rope, 2, 3), tm=512, tf=1024, layer=l,
                           pages_per_chunk=16, nbuf=4, new_tokens=t_new)
    o_s = jnp.transpose(o_sb.reshape(dbatch, HEADS, t_new, KV_RANK), (1, 0, 2, 3)).reshape(HEADS, srows, KV_RANK)
    x1_s, qm_s = _merge(xs, o_s, cm_s, w, None, tm=tm, tiles_per_batch=1)
    mem_rows = (depth * dbatch, mem_len * MEM_HEADS, MEM_HD)
    om_s = _smem(qm_s.reshape(dbatch, t_new * MEM_HEADS, MEM_HD), cache_mem_k.reshape(mem_rows),
                 cache_mem_v.reshape(mem_rows), layer=l, bb=8)
    y_s = _ffn(x1_s, om_s.reshape(srows, MEM_INNER), w, tm=512, tf=1024)

    mem_shape = (depth, batch, mem_len, MEM_HEADS, MEM_HD)
    return (y_p.reshape(batch, seq, d), y_s.reshape(dbatch, t_new, d),
            kvc_p.reshape(depth, batch, seq, KV_RANK), kpe_p.reshape(depth, batch, seq, QK_ROPE),
            mk_p.reshape(mem_shape), mv_p.reshape(mem_shape),
            kvc_s.reshape(depth, dbatch, t_new, KV_RANK), kpe_s.reshape(depth, dbatch, t_new, QK_ROPE),
            v_s.reshape(depth, dbatch, t_new, GROUPS, GW))
```

```python
import functools

import jax
import jax.numpy as jnp
from jax import lax
from jax.experimental import pallas as pl
from jax.experimental.pallas import tpu as pltpu

F32 = jnp.float32
BF16 = jnp.bfloat16

EPS = 1e-6
ROPE_THETA = 10000.0
PAST_LEN = 16384

LANES = 128
V7X_VMEM_BYTES = 64 * 2**20
VMEM_LIMIT = V7X_VMEM_BYTES - 8 * 2**20

Q_RANK = 512
KV_RANK = 256
QK_NOPE = 128
QK_ROPE = 64
HEADS = 8
V_HEAD = 128
GROUPS = 8
GW = 128
CHUNK = 128
CW = GROUPS * GW
AW = HEADS * V_HEAD
KW = KV_RANK + LANES
MEM_HEADS = 4
MEM_HD = 128
MEM_INNER = MEM_HEADS * MEM_HD
MLA_SCALE = (QK_NOPE + QK_ROPE) ** -0.5
MEM_SCALE = MEM_HD ** -0.5

O_Q = 0
O_KV = O_Q + Q_RANK
O_PE = O_KV + KV_RANK
O_U = O_PE + LANES
O_V = O_U + CW
D_IN_PAD = O_V + CW

NT_DIMS = (((1,), (1,)), ((), ()))


def _const_spec(shape):
    return pl.BlockSpec(shape, lambda *_: (0,) * len(shape), pipeline_mode=pl.Buffered(1))


def _params(semantics):
    return pltpu.CompilerParams(dimension_semantics=semantics, vmem_limit_bytes=VMEM_LIMIT)


def _rms(x, g):
    return x * lax.rsqrt(jnp.mean(x * x, axis=-1, keepdims=True) + EPS) * g


def _gelu(x):
    return 0.5 * x * (1.0 + jnp.tanh(0.7978845608028654 * (x + 0.044715 * (x * x * x))))


def _dot(a, b):
    return jnp.dot(a, b, preferred_element_type=F32)


def _dot_nt(a, b):
    return lax.dot_general(a, b, NT_DIMS, preferred_element_type=F32)


def _proj_kernel(x_ref, gmix_ref, win_ref, qg_ref, wq_ref, kvg_ref, wuk_ref, vg_ref,
                 cos_ref, s1_ref, s2_ref, wmix_ref, bmix_ref, cmg_ref, *rest, prompt):
    if prompt:
        wup32_ref, wdn32_ref, q_ref, kmat_ref, kvc_ref, kpe_ref, cm_ref, extra_ref, wup_ref, wdn_ref, cm_sc = rest
        wup_ref[...] = wup32_ref[...].astype(BF16)
        wdn_ref[...] = wdn32_ref[...].astype(BF16)
    else:
        q_ref, kmat_ref, kvc_ref, kpe_ref, cm_ref, extra_ref, cm_sc = rest
    tm = x_ref.shape[0]
    h = _rms(x_ref[...], gmix_ref[...]).astype(BF16)

    def seg(lo, hi):
        return _dot(h, win_ref[:, lo:hi])

    cosv, s1, s2 = cos_ref[...], s1_ref[...], s2_ref[...]

    def rope(p):
        return (p * cosv + pltpu.roll(p, LANES - QK_ROPE // 2, 1) * s1
                + pltpu.roll(p, QK_ROPE // 2, 1) * s2)

    qc = _rms(seg(O_Q, O_KV), qg_ref[...]).astype(BF16)
    q = _dot(qc, wq_ref[...])
    for hh in range(HEADS):
        base = hh * 2 * LANES
        qa = _dot(q[:, base:base + QK_NOPE].astype(BF16), wuk_ref[hh]) * MLA_SCALE
        qp = rope(q[:, base + LANES:base + 2 * LANES]) * MLA_SCALE
        if prompt:
            cols = slice(hh * tm, (hh + 1) * tm)
            q_ref[0:KV_RANK, cols] = qa.T.astype(BF16)
            q_ref[KV_RANK:KW, cols] = qp.T.astype(BF16)
        else:
            q_ref[hh, :, 0:KV_RANK] = qa.astype(BF16)
            q_ref[hh, :, KV_RANK:KW] = qp.astype(BF16)

    kv = _rms(seg(O_KV, O_PE), kvg_ref[...])
    kvc_ref[...] = kv
    kmat_ref[:, 0:KV_RANK] = kv.astype(BF16)
    if prompt:
        extra_ref[...] = kv.T.astype(BF16)
    kp = rope(seg(O_PE, O_U))
    kpe_ref[...] = kp[:, 0:QK_ROPE]
    kmat_ref[:, KV_RANK:KW] = kp.astype(BF16)

    u = _gelu(seg(O_U, O_V))
    v = _rms(_gelu(seg(O_V, D_IN_PAD)), vg_ref[...])
    if not prompt:
        extra_ref[...] = v
    vb = v.astype(BF16)
    row = lax.broadcasted_iota(jnp.int32, (CHUNK, CHUNK), 0)
    col = lax.broadcasted_iota(jnp.int32, (CHUNK, CHUNK), 1)
    for g in range(GROUPS):
        wg = jnp.where(row >= col, wmix_ref[g], 0.0).astype(BF16)
        bg = bmix_ref[:, g:g + 1]
        for c in range(tm // CHUNK):
            rs = slice(c * CHUNK, (c + 1) * CHUNK)
            cs = slice(g * GW, (g + 1) * GW)
            cm_sc[rs, cs] = u[rs, cs] * (_dot(wg, vb[rs, cs]) + bg)
    cm_ref[...] = _rms(cm_sc[...], cmg_ref[...]).astype(BF16)


def _proj(x, tabs, tab_blocks, wmix, bmix, w, ffn_weights=None, *, tm, prompt):
    rows, d = x.shape
    n = rows // tm
    cos_t, s1_t, s2_t = tabs
    tab_spec = pl.BlockSpec((tm, LANES), lambda i: (i % tab_blocks, 0))
    row_spec = lambda width: pl.BlockSpec((tm, width), lambda i: (i, 0))
    in_specs = [
        row_spec(d), _const_spec((1, d)), _const_spec((d, D_IN_PAD)), _const_spec((1, Q_RANK)),
        _const_spec((Q_RANK, HEADS * 2 * LANES)), _const_spec((1, KV_RANK)),
        _const_spec((HEADS, QK_NOPE, KV_RANK)), _const_spec((1, CW)),
        tab_spec, tab_spec, tab_spec,
        _const_spec((GROUPS, CHUNK, CHUNK)), _const_spec((CHUNK, GROUPS)), _const_spec((1, CW)),
    ]
    if prompt:
        q_shape, q_spec = (KW, HEADS * rows), pl.BlockSpec((KW, HEADS * tm), lambda i: (0, i))
        extra_shape = jax.ShapeDtypeStruct((KV_RANK, rows), BF16)
        extra_spec = pl.BlockSpec((KV_RANK, tm), lambda i: (0, i))
    else:
        q_shape, q_spec = (HEADS, rows, KW), pl.BlockSpec((HEADS, tm, KW), lambda i: (0, i, 0))
        extra_shape, extra_spec = jax.ShapeDtypeStruct((rows, CW), F32), row_spec(CW)
    out_shape = [
        jax.ShapeDtypeStruct(q_shape, BF16),
        jax.ShapeDtypeStruct((rows, KW), BF16),
        jax.ShapeDtypeStruct((rows, KV_RANK), F32),
        jax.ShapeDtypeStruct((rows, QK_ROPE), F32),
        jax.ShapeDtypeStruct((rows, CW), BF16),
        extra_shape,
    ]
    out_specs = [q_spec, row_spec(KW), row_spec(KV_RANK), row_spec(QK_ROPE), row_spec(CW), extra_spec]
    args = [x, w["gmix"], w["win"], w["qg"], w["wq"], w["kvg"], w["wuk"], w["vg"],
            cos_t, s1_t, s2_t, wmix, bmix, w["cmg"]]
    if prompt:
        wup32, wdn32 = ffn_weights
        slab = wup32.shape[1] // n
        up_spec = pl.BlockSpec((d, slab), lambda i: (0, i))
        dn_spec = pl.BlockSpec((slab, d), lambda i: (i, 0))
        in_specs += [up_spec, dn_spec]
        args += [wup32, wdn32]
        out_shape += [jax.ShapeDtypeStruct(wup32.shape, BF16), jax.ShapeDtypeStruct(wdn32.shape, BF16)]
        out_specs += [up_spec, dn_spec]
    return pl.pallas_call(
        functools.partial(_proj_kernel, prompt=prompt),
        grid=(n,), in_specs=in_specs, out_specs=out_specs, out_shape=out_shape,
        scratch_shapes=[pltpu.VMEM((tm, CW), F32)],
        compiler_params=_params(("arbitrary",)),
        name="proj_prompt" if prompt else "proj_sample",
    )(*args)


def _memkv_kernel(m_ref, g_ref, wk_ref, wv_ref, k_ref, v_ref):
    m = _rms(m_ref[...], g_ref[...]).astype(BF16)
    k_ref[...] = _dot(m, wk_ref[...])
    v_ref[...] = _dot(m, wv_ref[...])


def _memkv(mem, g, wk, wv, *, tm):
    rows, d = mem.shape
    row_spec = lambda width: pl.BlockSpec((tm, width), lambda i: (i, 0))
    return pl.pallas_call(
        _memkv_kernel, grid=(rows // tm,),
        in_specs=[row_spec(d), _const_spec((1, d)), _const_spec((d, MEM_INNER)),
                  _const_spec((d, MEM_INNER))],
        out_specs=[row_spec(MEM_INNER), row_spec(MEM_INNER)],
        out_shape=[jax.ShapeDtypeStruct((rows, MEM_INNER), F32)] * 2,
        compiler_params=_params(("arbitrary",)), name="memkv",
    )(mem, g, wk, wv)


def _online_step(s, v, m, l, acc):
    m_new = jnp.maximum(m, jnp.max(s, axis=-1, keepdims=True))
    p = jnp.exp(s - m_new)
    alpha = jnp.exp(m - m_new)
    return (m_new, alpha * l + jnp.sum(p, axis=-1, keepdims=True),
            alpha * acc + _dot(p.astype(v.dtype), v))


def _online_init(m_sc, l_sc, acc_sc):
    m_sc[...] = jnp.full(m_sc.shape, -jnp.inf, F32)
    l_sc[...] = jnp.zeros(l_sc.shape, F32)
    acc_sc[...] = jnp.zeros(acc_sc.shape, F32)


def _pattn_kernel(qt_ref, k_ref, kvt_ref, o_ref, m_sc, l_sc, acc_sc, sa_sc, sb_sc):
    i = pl.program_id(1)
    heads, tq, _ = o_ref.shape
    cols = heads * tq
    _online_init(m_sc, l_sc, acc_sc)

    def keys_of(j):
        return pl.ds(pl.multiple_of(j * tq, tq), tq)

    def scores(j, st_ref):
        st_ref[...] = _dot(k_ref[keys_of(j), :], qt_ref[...])

    def update(j, st_ref, masked):
        st = st_ref[...]
        if masked:
            key = lax.broadcasted_iota(jnp.int32, st.shape, 0)
            query = lax.broadcasted_iota(jnp.int32, st.shape, 1) % tq
            st = jnp.where(key <= query, st, -jnp.inf)
        m_prev = m_sc[...]
        m_new = jnp.maximum(m_prev, jnp.max(st, axis=0, keepdims=True))
        p = jnp.exp(st - m_new)
        alpha = jnp.exp(m_prev - m_new)
        l_sc[...] = alpha * l_sc[...] + jnp.sum(p, axis=0, keepdims=True)
        acc_sc[...] = alpha * acc_sc[...] + _dot(kvt_ref[:, keys_of(j)], p.astype(BF16))
        m_sc[...] = m_new

    scores(0, sa_sc)

    def block_pair(pair, carry):
        j = 2 * pair
        scores(j + 1, sb_sc)
        update(j, sa_sc, False)
        scores(j + 2, sa_sc)
        update(j + 1, sb_sc, False)
        return carry

    lax.fori_loop(0, i // 2, block_pair, 0)

    @pl.when(i % 2 == 0)
    def _():
        update(i, sa_sc, True)

    @pl.when(i % 2 == 1)
    def _():
        scores(i, sb_sc)
        update(i - 1, sa_sc, False)
        update(i, sb_sc, True)

    ot = acc_sc[...] * (1.0 / l_sc[...])
    for hh in range(heads):
        o_ref[hh] = ot[:, hh * tq:(hh + 1) * tq].T.astype(BF16)


def _pattn(qt, kmat, kvt, *, batch, seq, tq):
    heads = qt.shape[1] // (batch * seq)
    nq = seq // tq
    return pl.pallas_call(
        _pattn_kernel, grid=(batch, nq),
        in_specs=[pl.BlockSpec((KW, heads * tq), lambda b, i: (0, b * nq + i)),
                  pl.BlockSpec((seq, KW), lambda b, i: (b, 0)),
                  pl.BlockSpec((KV_RANK, seq), lambda b, i: (0, b))],
        out_specs=pl.BlockSpec((heads, tq, KV_RANK), lambda b, i: (0, b * nq + i, 0)),
        out_shape=jax.ShapeDtypeStruct((heads, batch * seq, KV_RANK), BF16),
        scratch_shapes=[pltpu.VMEM((1, heads * tq), F32), pltpu.VMEM((1, heads * tq), F32),
                        pltpu.VMEM((KV_RANK, heads * tq), F32),
                        pltpu.VMEM((tq, heads * tq), F32), pltpu.VMEM((tq, heads * tq), F32)],
        compiler_params=_params(("arbitrary", "arbitrary")), name="pattn",
    )(qt, kmat, kvt)


def _ffn_dattn_kernel(pt_ref, x2_ref, hf_ref, wup_ref, wdn_ref, gfin_ref,
                      q_ref, kn_ref, ckv_hbm, cpe_hbm, y_ref, o_ref, facc_sc, *scratch,
                      layer, pages_per_chunk, nbuf, steps_per_batch, new_tokens):
    assert nbuf == 4
    kvbufs, pebufs = scratch[0:nbuf], scratch[nbuf:2 * nbuf]
    sems, m_sc, l_sc, acc_sc = scratch[2 * nbuf:]
    f = pl.program_id(1)
    nf = pl.num_programs(1)
    n = pl.program_id(0) * nf + f
    last_step = n + 1 == pl.num_programs(0) * nf
    c = n % steps_per_batch
    page = kvbufs[0].shape[0] // pages_per_chunk
    pages_per_step = nbuf * pages_per_chunk
    first_page = n * pages_per_step

    def start_chunk(first, buf):
        for j in range(pages_per_chunk):
            page_id = pt_ref[first + j]
            keys = slice(j * page, (j + 1) * page)
            pltpu.make_async_copy(ckv_hbm.at[layer, page_id], kvbufs[buf].at[keys],
                                  sems.at[0, buf]).start(priority=j % 2)
            pltpu.make_async_copy(cpe_hbm.at[layer, page_id], pebufs[buf].at[:, keys],
                                  sems.at[1, buf]).start(priority=(j + 1) % 2)

    def wait_chunk(buf):
        pltpu.make_async_copy(kvbufs[buf], kvbufs[buf], sems.at[0, buf]).wait()
        pltpu.make_async_copy(pebufs[buf], pebufs[buf], sems.at[1, buf]).wait()

    @pl.when(n == 0)
    def _():
        _online_init(m_sc, l_sc, acc_sc)
        for buf in range(nbuf):
            start_chunk(first_page + buf * pages_per_chunk, buf)

    @pl.when(f == 0)
    def _():
        facc_sc[...] = x2_ref[...]

    q = q_ref[0]
    qa = q[:, 0:KV_RANK].astype(F32)
    qp = q[:, KV_RANK:KV_RANK + QK_ROPE].astype(F32)

    kn = kn_ref[0]
    sn = _dot_nt(q, kn)
    t = lax.broadcasted_iota(jnp.int32, sn.shape, 0) % new_tokens
    cidx = lax.broadcasted_iota(jnp.int32, sn.shape, 1)
    sn = jnp.where(cidx <= t, sn, -jnp.inf)
    m_new = jnp.max(sn, axis=-1, keepdims=True)
    p_new = jnp.exp(sn - m_new)
    first = c == 0
    state = (jnp.where(first, m_new, m_sc[...]),
             jnp.where(first, jnp.sum(p_new, axis=-1, keepdims=True), l_sc[...]),
             jnp.where(first, _dot(p_new.astype(BF16), kn[:, 0:KV_RANK]), acc_sc[...]))

    qrows = q.shape[0]
    qa_pad = jnp.concatenate([qa, jnp.zeros((LANES - qrows, KV_RANK), F32)], axis=0)

    def scores(buf):
        st = _dot_nt(kvbufs[buf][...], qa_pad)
        return st.T[0:qrows] + _dot(qp, pebufs[buf][...])

    next_first = jnp.where(last_step, 0, first_page + pages_per_step)

    def attend(s, buf, state):
        state = _online_step(s, kvbufs[buf][...], *state)
        start_chunk(next_first + buf * pages_per_chunk, buf)
        return state

    half_d = wdn_ref.shape[1] // 2

    wait_chunk(0)
    wait_chunk(1)
    s0 = scores(0)
    s1 = scores(1)
    state = attend(s0, 0, state)
    a = jnp.maximum(_dot(hf_ref[...], wup_ref[...]), 0.0)
    state = attend(s1, 1, state)
    a = (a * a).astype(BF16)

    wait_chunk(2)
    wait_chunk(3)
    s2 = scores(2)
    s3 = scores(3)
    state = attend(s2, 2, state)
    facc_sc[:, 0:half_d] += _dot(a, wdn_ref[:, 0:half_d])
    state = attend(s3, 3, state)
    facc_sc[:, half_d:] += _dot(a, wdn_ref[:, half_d:])
    m_sc[...], l_sc[...], acc_sc[...] = state

    @pl.when(last_step)
    def _():
        for buf in range(nbuf):
            wait_chunk(buf)

    @pl.when(c == steps_per_batch - 1)
    def _():
        o_ref[0] = (state[2] * (1.0 / state[1])).astype(BF16)

    @pl.when(f == nf - 1)
    def _():
        y_ref[...] = _rms(facc_sc[...], gfin_ref[...])


def _ffn_dattn(x2, hf, wup, wdn, gfin, page_table, q, knew, cache_kv, cache_pe, *, tm, tf, layer,
               pages_per_chunk, nbuf, new_tokens):
    rows, d = x2.shape
    d_ff = wup.shape[1]
    nbatch, qrows, _ = q.shape
    n_pages = page_table.shape[1]
    page = cache_kv.shape[2]
    keys = pages_per_chunk * page
    ni, nf = rows // tm, d_ff // tf
    steps_per_batch = n_pages // (nbuf * pages_per_chunk)
    assert ni * nf == nbatch * steps_per_batch
    row_spec = lambda width: pl.BlockSpec((tm, width), lambda i, f, pt: (i, 0))
    batch_spec = lambda r, width: pl.BlockSpec(
        (1, r, width), lambda i, f, pt: ((i * nf + f) // steps_per_batch, 0, 0))
    grid_spec = pltpu.PrefetchScalarGridSpec(
        num_scalar_prefetch=1, grid=(ni, nf),
        in_specs=[row_spec(d), row_spec(d),
                  pl.BlockSpec((d, tf), lambda i, f, pt: (0, f)),
                  pl.BlockSpec((tf, d), lambda i, f, pt: (f, 0)), _const_spec((1, d)),
                  batch_spec(qrows, KW), batch_spec(LANES, KW),
                  pl.BlockSpec(memory_space=pl.ANY), pl.BlockSpec(memory_space=pl.ANY)],
        out_specs=[row_spec(d), batch_spec(qrows, KV_RANK)],
        scratch_shapes=([pltpu.VMEM((tm, d), F32)]
                        + [pltpu.VMEM((keys, KV_RANK), F32)] * nbuf + [pltpu.VMEM((QK_ROPE, keys), F32)] * nbuf
                        + [pltpu.SemaphoreType.DMA((2, nbuf)),
                           pltpu.VMEM((qrows, 1), F32), pltpu.VMEM((qrows, 1), F32),
                           pltpu.VMEM((qrows, KV_RANK), F32)]))
    return pl.pallas_call(
        functools.partial(_ffn_dattn_kernel, layer=layer, pages_per_chunk=pages_per_chunk, nbuf=nbuf,
                          steps_per_batch=steps_per_batch, new_tokens=new_tokens),
        grid_spec=grid_spec,
        out_shape=[jax.ShapeDtypeStruct((rows, d), F32), jax.ShapeDtypeStruct((nbatch, qrows, KV_RANK), BF16)],
        compiler_params=_params(("arbitrary", "arbitrary")), name="ffn_dattn",
    )(page_table.reshape(-1), x2, hf, wup, wdn, gfin, q, knew, cache_kv, cache_pe)


def _merge_kernel(x_ref, o_ref, cm_ref, wuv_ref, ag_ref, wout_ref, gmem_ref, wmq_ref, *rest, fuse_mem):
    if fuse_mem:
        mk_ref, mv_ref, wmo_ref, gffn_ref, xo_ref, ho_ref, attn_sc, om_sc = rest
    else:
        xo_ref, ho_ref, attn_sc = rest
    for hh in range(HEADS):
        attn_sc[:, hh * V_HEAD:(hh + 1) * V_HEAD] = _dot(o_ref[hh], wuv_ref[hh])
    attn_n = _rms(attn_sc[...], ag_ref[...]).astype(BF16)
    mix = _dot(attn_n, wout_ref[0:AW, :]) + _dot(cm_ref[...], wout_ref[AW:AW + CW, :])
    x1 = x_ref[...] + mix
    qm = _dot(_rms(x1, gmem_ref[...]).astype(BF16), wmq_ref[...]) * MEM_SCALE
    if not fuse_mem:
        xo_ref[...] = x1
        ho_ref[...] = qm.astype(BF16)
        return
    for hh in range(MEM_HEADS):
        cs = slice(hh * MEM_HD, (hh + 1) * MEM_HD)
        s = _dot_nt(qm[:, cs].astype(BF16), mk_ref[0, :, cs])
        p = jnp.exp(s - jnp.max(s, axis=-1, keepdims=True))
        oh = _dot(p.astype(BF16), mv_ref[0, :, cs])
        om_sc[:, cs] = (oh * (1.0 / jnp.sum(p, axis=-1, keepdims=True))).astype(BF16)
    x2 = x1 + _dot(om_sc[...], wmo_ref[...])
    xo_ref[...] = x2
    ho_ref[...] = _rms(x2, gffn_ref[...]).astype(BF16)


def _merge(x, o_lat, cm, w, mem_kv, *, tm, tiles_per_batch):
    rows, d = x.shape
    fuse_mem = mem_kv is not None
    row_spec = lambda width: pl.BlockSpec((tm, width), lambda i: (i, 0))
    in_specs = [row_spec(d), pl.BlockSpec((HEADS, tm, KV_RANK), lambda i: (0, i, 0)), row_spec(CW),
                _const_spec((HEADS, KV_RANK, V_HEAD)), _const_spec((1, AW)),
                _const_spec((AW + CW, d)), _const_spec((1, d)), _const_spec((d, MEM_INNER))]
    args = [x, o_lat, cm, w["wuv"], w["ag"], w["wout"], w["gmem"], w["wmq"]]
    scratch = [pltpu.VMEM((tm, AW), F32)]
    second_width = MEM_INNER
    if fuse_mem:
        mem_len = mem_kv[0].shape[1]
        mem_spec = pl.BlockSpec((1, mem_len, MEM_INNER), lambda i: (i // tiles_per_batch, 0, 0))
        in_specs += [mem_spec, mem_spec, _const_spec((MEM_INNER, d)), _const_spec((1, d))]
        args += list(mem_kv) + [w["wmo"], w["gffn"]]
        scratch.append(pltpu.VMEM((tm, MEM_INNER), BF16))
        second_width = d
    return pl.pallas_call(
        functools.partial(_merge_kernel, fuse_mem=fuse_mem), grid=(rows // tm,),
        in_specs=in_specs, out_specs=[row_spec(d), row_spec(second_width)],
        out_shape=[jax.ShapeDtypeStruct((rows, d), F32), jax.ShapeDtypeStruct((rows, second_width), BF16)],
        scratch_shapes=scratch,
        compiler_params=_params(("arbitrary",)), name="merge_mem" if fuse_mem else "merge",
    )(*args)


def _smem_kernel(q_ref, mk_ref, mv_ref, o_ref):
    rows, mrows = q_ref.shape[1], mk_ref.shape[1]
    qhead = lax.broadcasted_iota(jnp.int32, (rows, mrows), 0) % MEM_HEADS
    mhead = lax.broadcasted_iota(jnp.int32, (rows, mrows), 1) % MEM_HEADS
    same_head = qhead == mhead

    def body(bi, carry):
        s = _dot_nt(q_ref[bi], mk_ref[bi].astype(BF16))
        s = jnp.where(same_head, s, -jnp.inf)
        p = jnp.exp(s - jnp.max(s, axis=-1, keepdims=True))
        o = _dot(p.astype(BF16), mv_ref[bi].astype(BF16))
        o_ref[bi] = (o * (1.0 / jnp.sum(p, axis=-1, keepdims=True))).astype(BF16)
        return carry
    lax.fori_loop(0, q_ref.shape[0], body, 0)


def _smem(q, mk, mv, *, layer, bb):
    nbatch, rows, _ = q.shape
    mrows = mk.shape[1]
    nblk = nbatch // bb
    q_spec = pl.BlockSpec((bb, rows, MEM_HD), lambda i: (i, 0, 0))
    m_spec = pl.BlockSpec((bb, mrows, MEM_HD), lambda i: (layer * nblk + i, 0, 0))
    return pl.pallas_call(
        _smem_kernel, grid=(nblk,), in_specs=[q_spec, m_spec, m_spec], out_specs=q_spec,
        out_shape=jax.ShapeDtypeStruct(q.shape, BF16),
        compiler_params=_params(("arbitrary",)), name="smem",
    )(q, mk, mv)


def _ffn_kernel(x1_ref, om_ref, wmo_ref, gffn_ref, wup_ref, wdn_ref, gfin_ref, y_ref, hf_sc, acc_sc):
    f = pl.program_id(1)

    @pl.when(f == 0)
    def _():
        x2 = x1_ref[...] + _dot(om_ref[...], wmo_ref[...])
        acc_sc[...] = x2
        hf_sc[...] = _rms(x2, gffn_ref[...]).astype(BF16)

    a = jnp.maximum(_dot(hf_sc[...], wup_ref[...]), 0.0)
    acc_sc[...] += _dot((a * a).astype(BF16), wdn_ref[...])

    @pl.when(f == pl.num_programs(1) - 1)
    def _():
        y_ref[...] = _rms(acc_sc[...], gfin_ref[...])


def _ffn(x1, om, w, *, tm, tf):
    rows, d = x1.shape
    d_ff = w["wup"].shape[1]
    row_spec = lambda width: pl.BlockSpec((tm, width), lambda i, f: (i, 0))
    return pl.pallas_call(
        _ffn_kernel, grid=(rows // tm, d_ff // tf),
        in_specs=[row_spec(d), row_spec(MEM_INNER), _const_spec((MEM_INNER, d)), _const_spec((1, d)),
                  pl.BlockSpec((d, tf), lambda i, f: (0, f)), pl.BlockSpec((tf, d), lambda i, f: (f, 0)),
                  _const_spec((1, d))],
        out_specs=row_spec(d), out_shape=jax.ShapeDtypeStruct((rows, d), F32),
        scratch_shapes=[pltpu.VMEM((tm, d), BF16), pltpu.VMEM((tm, d), F32)],
        compiler_params=_params(("arbitrary", "arbitrary")), name="ffn",
    )(x1, om, w["wmo"], w["gffn"], w["wup"], w["wdn"], w["gfin"])


def _rope_tables(pos):
    half = QK_ROPE // 2
    inv = ROPE_THETA ** (-jnp.arange(half, dtype=F32) / half)
    ang = pos.astype(F32)[:, None] * inv[None, :]
    cos, sin, z = jnp.cos(ang), jnp.sin(ang), jnp.zeros_like(ang)
    return (jnp.concatenate([cos, cos, z, z], axis=1),
            jnp.concatenate([-sin, z, z, z], axis=1),
            jnp.concatenate([z, sin, z, z], axis=1))


def _layer_weights(l, norm_mix_g, w_in, q_norm_g, w_q_up, kv_norm_g, w_uk, w_uv, v_norm_g, attn_out_g,
                   cmlp_out_g, w_out, norm_mem_g, w_mem_q, w_mem_o, norm_ffn_g, w_ffn_up, w_ffn_down,
                   final_norm_g):
    d = w_in.shape[1]
    o_pe_end = Q_RANK + KV_RANK + QK_ROPE
    win = jnp.concatenate([w_in[l, :, :o_pe_end], jnp.zeros((d, LANES - QK_ROPE), F32),
                           w_in[l, :, o_pe_end:]], axis=1).astype(BF16)
    wq = w_q_up[l].reshape(Q_RANK, HEADS, QK_NOPE + QK_ROPE)
    wq = jnp.concatenate([wq, jnp.zeros((Q_RANK, HEADS, 2 * LANES - QK_NOPE - QK_ROPE), F32)], axis=2)
    row = lambda g: g.reshape(1, -1)
    return dict(
        gmix=row(norm_mix_g[l]), win=win, qg=row(q_norm_g[l]),
        wq=wq.reshape(Q_RANK, HEADS * 2 * LANES).astype(BF16), kvg=row(kv_norm_g[l]),
        wuk=jnp.transpose(w_uk[l], (1, 2, 0)).astype(BF16), vg=row(v_norm_g[l]),
        cmg=row(cmlp_out_g[l]), wuv=jnp.transpose(w_uv[l], (1, 0, 2)).astype(BF16),
        ag=row(attn_out_g[l]), wout=w_out[l].astype(BF16), gmem=row(norm_mem_g[l]),
        wmq=w_mem_q[l].astype(BF16), wmo=w_mem_o[l].astype(BF16), gffn=row(norm_ffn_g[l]),
        wup32=w_ffn_up[l], wdn32=w_ffn_down[l], gfin=row(final_norm_g))


def kernel(x_prompt, x_sample, cache_kv_latent, cache_k_rope, cache_mem_k, cache_mem_v, page_table,
           mem_prompt, norm_mix_g, w_in, q_norm_g, w_q_up, kv_norm_g, w_uk, w_uv, v_norm_g, w_spatial,
           b_spatial, attn_out_g, cmlp_out_g, w_out, norm_mem_g, mem_in_g, w_mem_q, w_mem_k, w_mem_v,
           w_mem_o, norm_ffn_g, w_ffn_up, w_ffn_down, final_norm_g):
    batch, seq, d = x_prompt.shape
    dbatch, t_new, _ = x_sample.shape
    depth = w_in.shape[0]
    assert depth == 1 and seq % CHUNK == 0 and CHUNK % t_new == 0
    l = 0
    mem_len = mem_prompt.shape[1]
    tm = 256
    srows = dbatch * t_new

    w = _layer_weights(l, norm_mix_g, w_in, q_norm_g, w_q_up, kv_norm_g, w_uk, w_uv, v_norm_g,
                       attn_out_g, cmlp_out_g, w_out, norm_mem_g, w_mem_q, w_mem_o, norm_ffn_g,
                       w_ffn_up, w_ffn_down, final_norm_g)

    xp = x_prompt.reshape(batch * seq, d)
    tabs_p = _rope_tables(jnp.arange(seq))
    qt_p, kmat_p, kvc_p, kpe_p, cm_p, kvt_p, w["wup"], w["wdn"] = _proj(
        xp, tabs_p, seq // tm, w_spatial[l], b_spatial[l].T, w, (w["wup32"], w["wdn32"]),
        tm=tm, prompt=True)
    mk_p, mv_p = _memkv(mem_prompt.reshape(batch * mem_len, d), mem_in_g[l].reshape(1, -1),
                        w_mem_k[l].astype(BF16), w_mem_v[l].astype(BF16), tm=tm)
    o_p = _pattn(qt_p, kmat_p, kvt_p, batch=batch, seq=seq, tq=tm)
    mem_kv_p = (mk_p.astype(BF16).reshape(batch, mem_len, MEM_INNER),
                mv_p.astype(BF16).reshape(batch, mem_len, MEM_INNER))
    x2_p, hf_p = _merge(xp, o_p, cm_p, w, mem_kv_p, tm=tm, tiles_per_batch=seq // tm)

    xs = x_sample.reshape(srows, d)
    reps = CHUNK // t_new
    tabs_s = _rope_tables(PAST_LEN + jnp.arange(tm) % t_new)
    eye = jnp.eye(reps, dtype=F32)
    wmix_s = jnp.einsum("ab,gts->gatbs", eye, w_spatial[l][:, :t_new, :t_new]).reshape(GROUPS, CHUNK, CHUNK)
    bmix_s = jnp.tile(b_spatial[l][:, :t_new].T, (reps, 1))
    q_s, kmat_s, kvc_s, kpe_s, cm_s, v_s = _proj(xs, tabs_s, 1, wmix_s, bmix_s, w, tm=tm, prompt=False)

    q_sb = jnp.transpose(q_s.reshape(HEADS, dbatch, t_new, KW), (1, 0, 2, 3)).reshape(dbatch, HEADS * t_new, KW)
    knew = jnp.pad(kmat_s.reshape(dbatch, t_new, KW), ((0, 0), (0, LANES - t_new), (0, 0)))
    y_p, o_sb = _ffn_dattn(x2_p, hf_p, w["wup"], w["wdn"], w["gfin"], page_table, q_sb, knew,
                           cache_kv_latent, jnp.swapaxes(cache_k_rope, 2, 3), tm=512, tf=512, layer=l,
                           pages_per_chunk=16, nbuf=4, new_tokens=t_new)
    o_s = jnp.transpose(o_sb.reshape(dbatch, HEADS, t_new, KV_RANK), (1, 0, 2, 3)).reshape(HEADS, srows, KV_RANK)
    x1_s, qm_s = _merge(xs, o_s, cm_s, w, None, tm=tm, tiles_per_batch=1)
    mem_rows = (depth * dbatch, mem_len * MEM_HEADS, MEM_HD)
    om_s = _smem(qm_s.reshape(dbatch, t_new * MEM_HEADS, MEM_HD), cache_mem_k.reshape(mem_rows),
                 cache_mem_v.reshape(mem_rows), layer=l, bb=8)
    y_s = _ffn(x1_s, om_s.reshape(srows, MEM_INNER), w, tm=512, tf=1024)

    mem_shape = (depth, batch, mem_len, MEM_HEADS, MEM_HD)
    return (y_p.reshape(batch, seq, d), y_s.reshape(dbatch, t_new, d),
            kvc_p.reshape(depth, batch, seq, KV_RANK), kpe_p.reshape(depth, batch, seq, QK_ROPE),
            mk_p.reshape(mem_shape), mv_p.reshape(mem_shape),
            kvc_s.reshape(depth, dbatch, t_new, KV_RANK), kpe_s.reshape(depth, dbatch, t_new, QK_ROPE),
            v_s.reshape(depth, dbatch, t_new, GROUPS, GW))
```

```python
import functools

import jax
import jax.numpy as jnp
from jax import lax
from jax.experimental import pallas as pl
from jax.experimental.pallas import tpu as pltpu

F32 = jnp.float32
BF16 = jnp.bfloat16

EPS = 1e-6
ROPE_THETA = 10000.0
PAST_LEN = 16384

LANES = 128
V7X_VMEM_BYTES = 64 * 2**20
VMEM_LIMIT = V7X_VMEM_BYTES - 8 * 2**20

Q_RANK = 512
KV_RANK = 256
QK_NOPE = 128
QK_ROPE = 64
HEADS = 8
V_HEAD = 128
GROUPS = 8
GW = 128
CHUNK = 128
CW = GROUPS * GW
AW = HEADS * V_HEAD
KW = KV_RANK + LANES
MEM_HEADS = 4
MEM_HD = 128
MEM_INNER = MEM_HEADS * MEM_HD
MLA_SCALE = (QK_NOPE + QK_ROPE) ** -0.5
MEM_SCALE = MEM_HD ** -0.5

O_Q = 0
O_KV = O_Q + Q_RANK
O_PE = O_KV + KV_RANK
O_U = O_PE + LANES
O_V = O_U + CW
D_IN_PAD = O_V + CW

NT_DIMS = (((1,), (1,)), ((), ()))


def _const_spec(shape):
    return pl.BlockSpec(shape, lambda *_: (0,) * len(shape), pipeline_mode=pl.Buffered(1))


def _params(semantics):
    return pltpu.CompilerParams(dimension_semantics=semantics, vmem_limit_bytes=VMEM_LIMIT)


def _rms(x, g):
    return x * lax.rsqrt(jnp.mean(x * x, axis=-1, keepdims=True) + EPS) * g


def _gelu(x):
    return 0.5 * x * (1.0 + jnp.tanh(0.7978845608028654 * (x + 0.044715 * (x * x * x))))


def _dot(a, b):
    return jnp.dot(a, b, preferred_element_type=F32)


def _dot_nt(a, b):
    return lax.dot_general(a, b, NT_DIMS, preferred_element_type=F32)


def _proj_kernel(x_ref, gmix_ref, win_ref, qg_ref, wq_ref, kvg_ref, wuk_ref, vg_ref,
                 cos_ref, s1_ref, s2_ref, wmix_ref, bmix_ref, cmg_ref, *rest, prompt):
    if prompt:
        wup32_ref, wdn32_ref, q_ref, kmat_ref, kvc_ref, kpe_ref, cm_ref, extra_ref, wup_ref, wdn_ref, cm_sc = rest
        wup_ref[...] = wup32_ref[...].astype(BF16)
        wdn_ref[...] = wdn32_ref[...].astype(BF16)
    else:
        q_ref, kmat_ref, kvc_ref, kpe_ref, cm_ref, extra_ref, cm_sc = rest
    tm = x_ref.shape[0]
    h = _rms(x_ref[...], gmix_ref[...]).astype(BF16)

    def seg(lo, hi):
        return _dot(h, win_ref[:, lo:hi])

    cosv, s1, s2 = cos_ref[...], s1_ref[...], s2_ref[...]

    def rope(p):
        return (p * cosv + pltpu.roll(p, LANES - QK_ROPE // 2, 1) * s1
                + pltpu.roll(p, QK_ROPE // 2, 1) * s2)

    qc = _rms(seg(O_Q, O_KV), qg_ref[...]).astype(BF16)
    q = _dot(qc, wq_ref[...])
    for hh in range(HEADS):
        base = hh * 2 * LANES
        qa = _dot(q[:, base:base + QK_NOPE].astype(BF16), wuk_ref[hh]) * MLA_SCALE
        qp = rope(q[:, base + LANES:base + 2 * LANES]) * MLA_SCALE
        if prompt:
            cols = slice(hh * tm, (hh + 1) * tm)
            q_ref[0:KV_RANK, cols] = qa.T.astype(BF16)
            q_ref[KV_RANK:KW, cols] = qp.T.astype(BF16)
        else:
            q_ref[hh, :, 0:KV_RANK] = qa.astype(BF16)
            q_ref[hh, :, KV_RANK:KW] = qp.astype(BF16)

    kv = _rms(seg(O_KV, O_PE), kvg_ref[...])
    kvc_ref[...] = kv
    kmat_ref[:, 0:KV_RANK] = kv.astype(BF16)
    if prompt:
        extra_ref[...] = kv.T.astype(BF16)
    kp = rope(seg(O_PE, O_U))
    kpe_ref[...] = kp[:, 0:QK_ROPE]
    kmat_ref[:, KV_RANK:KW] = kp.astype(BF16)

    u = _gelu(seg(O_U, O_V))
    v = _rms(_gelu(seg(O_V, D_IN_PAD)), vg_ref[...])
    if not prompt:
        extra_ref[...] = v
    vb = v.astype(BF16)
    row = lax.broadcasted_iota(jnp.int32, (CHUNK, CHUNK), 0)
    col = lax.broadcasted_iota(jnp.int32, (CHUNK, CHUNK), 1)
    for g in range(GROUPS):
        wg = jnp.where(row >= col, wmix_ref[g], 0.0).astype(BF16)
        bg = bmix_ref[:, g:g + 1]
        for c in range(tm // CHUNK):
            rs = slice(c * CHUNK, (c + 1) * CHUNK)
            cs = slice(g * GW, (g + 1) * GW)
            cm_sc[rs, cs] = u[rs, cs] * (_dot(wg, vb[rs, cs]) + bg)
    cm_ref[...] = _rms(cm_sc[...], cmg_ref[...]).astype(BF16)


def _proj(x, tabs, tab_blocks, wmix, bmix, w, ffn_weights=None, *, tm, prompt):
    rows, d = x.shape
    n = rows // tm
    cos_t, s1_t, s2_t = tabs
    tab_spec = pl.BlockSpec((tm, LANES), lambda i: (i % tab_blocks, 0))
    row_spec = lambda width: pl.BlockSpec((tm, width), lambda i: (i, 0))
    in_specs = [
        row_spec(d), _const_spec((1, d)), _const_spec((d, D_IN_PAD)), _const_spec((1, Q_RANK)),
        _const_spec((Q_RANK, HEADS * 2 * LANES)), _const_spec((1, KV_RANK)),
        _const_spec((HEADS, QK_NOPE, KV_RANK)), _const_spec((1, CW)),
        tab_spec, tab_spec, tab_spec,
        _const_spec((GROUPS, CHUNK, CHUNK)), _const_spec((CHUNK, GROUPS)), _const_spec((1, CW)),
    ]
    if prompt:
        q_shape, q_spec = (KW, HEADS * rows), pl.BlockSpec((KW, HEADS * tm), lambda i: (0, i))
        extra_shape = jax.ShapeDtypeStruct((KV_RANK, rows), BF16)
        extra_spec = pl.BlockSpec((KV_RANK, tm), lambda i: (0, i))
    else:
        q_shape, q_spec = (HEADS, rows, KW), pl.BlockSpec((HEADS, tm, KW), lambda i: (0, i, 0))
        extra_shape, extra_spec = jax.ShapeDtypeStruct((rows, CW), F32), row_spec(CW)
    out_shape = [
        jax.ShapeDtypeStruct(q_shape, BF16),
        jax.ShapeDtypeStruct((rows, KW), BF16),
        jax.ShapeDtypeStruct((rows, KV_RANK), F32),
        jax.ShapeDtypeStruct((rows, QK_ROPE), F32),
        jax.ShapeDtypeStruct((rows, CW), BF16),
        extra_shape,
    ]
    out_specs = [q_spec, row_spec(KW), row_spec(KV_RANK), row_spec(QK_ROPE), row_spec(CW), extra_spec]
    args = [x, w["gmix"], w["win"], w["qg"], w["wq"], w["kvg"], w["wuk"], w["vg"],
            cos_t, s1_t, s2_t, wmix, bmix, w["cmg"]]
    if prompt:
        wup32, wdn32 = ffn_weights
        slab = wup32.shape[1] // n
        up_spec = pl.BlockSpec((d, slab), lambda i: (0, i))
        dn_spec = pl.BlockSpec((slab, d), lambda i: (i, 0))
        in_specs += [up_spec, dn_spec]
        args += [wup32, wdn32]
        out_shape += [jax.ShapeDtypeStruct(wup32.shape, BF16), jax.ShapeDtypeStruct(wdn32.shape, BF16)]
        out_specs += [up_spec, dn_spec]
    return pl.pallas_call(
        functools.partial(_proj_kernel, prompt=prompt),
        grid=(n,), in_specs=in_specs, out_specs=out_specs, out_shape=out_shape,
        scratch_shapes=[pltpu.VMEM((tm, CW), F32)],
        compiler_params=_params(("arbitrary",)),
        name="proj_prompt" if prompt else "proj_sample",
    )(*args)


def _memkv_kernel(m_ref, g_ref, wk_ref, wv_ref, k_ref, v_ref):
    m = _rms(m_ref[...], g_ref[...]).astype(BF16)
    k_ref[...] = _dot(m, wk_ref[...])
    v_ref[...] = _dot(m, wv_ref[...])


def _memkv(mem, g, wk, wv, *, tm):
    rows, d = mem.shape
    row_spec = lambda width: pl.BlockSpec((tm, width), lambda i: (i, 0))
    return pl.pallas_call(
        _memkv_kernel, grid=(rows // tm,),
        in_specs=[row_spec(d), _const_spec((1, d)), _const_spec((d, MEM_INNER)),
                  _const_spec((d, MEM_INNER))],
        out_specs=[row_spec(MEM_INNER), row_spec(MEM_INNER)],
        out_shape=[jax.ShapeDtypeStruct((rows, MEM_INNER), F32)] * 2,
        compiler_params=_params(("arbitrary",)), name="memkv",
    )(mem, g, wk, wv)


def _online_step(s, v, m, l, acc):
    m_new = jnp.maximum(m, jnp.max(s, axis=-1, keepdims=True))
    p = jnp.exp(s - m_new)
    alpha = jnp.exp(m - m_new)
    return (m_new, alpha * l + jnp.sum(p, axis=-1, keepdims=True),
            alpha * acc + _dot(p.astype(v.dtype), v))


def _online_init(m_sc, l_sc, acc_sc):
    m_sc[...] = jnp.full(m_sc.shape, -jnp.inf, F32)
    l_sc[...] = jnp.zeros(l_sc.shape, F32)
    acc_sc[...] = jnp.zeros(acc_sc.shape, F32)


def _pattn_kernel(qt_ref, k_ref, kvt_ref, o_ref, m_sc, l_sc, acc_sc, sa_sc, sb_sc):
    i = pl.program_id(1)
    heads, tq, _ = o_ref.shape
    cols = heads * tq
    _online_init(m_sc, l_sc, acc_sc)

    def keys_of(j):
        return pl.ds(pl.multiple_of(j * tq, tq), tq)

    def scores(j, st_ref):
        st_ref[...] = _dot(k_ref[keys_of(j), :], qt_ref[...])

    def update(j, st_ref, masked):
        st = st_ref[...]
        if masked:
            key = lax.broadcasted_iota(jnp.int32, st.shape, 0)
            query = lax.broadcasted_iota(jnp.int32, st.shape, 1) % tq
            st = jnp.where(key <= query, st, -jnp.inf)
        m_prev = m_sc[...]
        m_new = jnp.maximum(m_prev, jnp.max(st, axis=0, keepdims=True))
        p = jnp.exp(st - m_new)
        alpha = jnp.exp(m_prev - m_new)
        l_sc[...] = alpha * l_sc[...] + jnp.sum(p, axis=0, keepdims=True)
        acc_sc[...] = alpha * acc_sc[...] + _dot(kvt_ref[:, keys_of(j)], p.astype(BF16))
        m_sc[...] = m_new

    scores(0, sa_sc)

    def block_pair(pair, carry):
        j = 2 * pair
        scores(j + 1, sb_sc)
        update(j, sa_sc, False)
        scores(j + 2, sa_sc)
        update(j + 1, sb_sc, False)
        return carry

    lax.fori_loop(0, i // 2, block_pair, 0)

    @pl.when(i % 2 == 0)
    def _():
        update(i, sa_sc, True)

    @pl.when(i % 2 == 1)
    def _():
        scores(i, sb_sc)
        update(i - 1, sa_sc, False)
        update(i, sb_sc, True)

    ot = acc_sc[...] * (1.0 / l_sc[...])
    for hh in range(heads):
        o_ref[hh] = ot[:, hh * tq:(hh + 1) * tq].T.astype(BF16)


def _pattn(qt, kmat, kvt, *, batch, seq, tq):
    heads = qt.shape[1] // (batch * seq)
    nq = seq // tq
    return pl.pallas_call(
        _pattn_kernel, grid=(batch, nq),
        in_specs=[pl.BlockSpec((KW, heads * tq), lambda b, i: (0, b * nq + i)),
                  pl.BlockSpec((seq, KW), lambda b, i: (b, 0)),
                  pl.BlockSpec((KV_RANK, seq), lambda b, i: (0, b))],
        out_specs=pl.BlockSpec((heads, tq, KV_RANK), lambda b, i: (0, b * nq + i, 0)),
        out_shape=jax.ShapeDtypeStruct((heads, batch * seq, KV_RANK), BF16),
        scratch_shapes=[pltpu.VMEM((1, heads * tq), F32), pltpu.VMEM((1, heads * tq), F32),
                        pltpu.VMEM((KV_RANK, heads * tq), F32),
                        pltpu.VMEM((tq, heads * tq), F32), pltpu.VMEM((tq, heads * tq), F32)],
        compiler_params=_params(("arbitrary", "arbitrary")), name="pattn",
    )(qt, kmat, kvt)


def _ffn_dattn_kernel(pt_ref, x2_ref, hf_ref, wup_ref, wdn_ref, gfin_ref,
                      q_ref, kn_ref, ckv_hbm, cpe_hbm, y_ref, o_ref, facc_sc, *scratch,
                      layer, pages_per_chunk, nbuf, steps_per_batch, new_tokens):
    assert nbuf == 4
    kvbufs, pebufs = scratch[0:nbuf], scratch[nbuf:2 * nbuf]
    sems, m_sc, l_sc, acc_sc = scratch[2 * nbuf:]
    f = pl.program_id(1)
    nf = pl.num_programs(1)
    n = pl.program_id(0) * nf + f
    last_step = n + 1 == pl.num_programs(0) * nf
    c = n % steps_per_batch
    page = kvbufs[0].shape[0] // pages_per_chunk
    pages_per_step = nbuf * pages_per_chunk
    first_page = n * pages_per_step

    def start_chunk(first, buf):
        for j in range(pages_per_chunk):
            page_id = pt_ref[first + j]
            keys = slice(j * page, (j + 1) * page)
            pltpu.make_async_copy(ckv_hbm.at[layer, page_id], kvbufs[buf].at[keys], sems.at[0, buf]).start()
            pltpu.make_async_copy(cpe_hbm.at[layer, page_id], pebufs[buf].at[:, keys],
                                  sems.at[1, buf]).start(priority=1)

    def wait_chunk(buf):
        pltpu.make_async_copy(kvbufs[buf], kvbufs[buf], sems.at[0, buf]).wait()
        pltpu.make_async_copy(pebufs[buf], pebufs[buf], sems.at[1, buf]).wait()

    @pl.when(n == 0)
    def _():
        _online_init(m_sc, l_sc, acc_sc)
        for buf in range(nbuf):
            start_chunk(first_page + buf * pages_per_chunk, buf)

    @pl.when(f == 0)
    def _():
        facc_sc[...] = x2_ref[...]

    q = q_ref[0]
    qa = q[:, 0:KV_RANK].astype(F32)
    qp = q[:, KV_RANK:KV_RANK + QK_ROPE].astype(F32)

    kn = kn_ref[0]
    sn = _dot_nt(q, kn)
    t = lax.broadcasted_iota(jnp.int32, sn.shape, 0) % new_tokens
    cidx = lax.broadcasted_iota(jnp.int32, sn.shape, 1)
    sn = jnp.where(cidx <= t, sn, -jnp.inf)
    m_new = jnp.max(sn, axis=-1, keepdims=True)
    p_new = jnp.exp(sn - m_new)
    first = c == 0
    state = (jnp.where(first, m_new, m_sc[...]),
             jnp.where(first, jnp.sum(p_new, axis=-1, keepdims=True), l_sc[...]),
             jnp.where(first, _dot(p_new.astype(BF16), kn[:, 0:KV_RANK]), acc_sc[...]))

    qrows = q.shape[0]
    qa_pad = jnp.concatenate([qa, jnp.zeros((LANES - qrows, KV_RANK), F32)], axis=0)

    def scores(buf):
        st = _dot_nt(kvbufs[buf][...], qa_pad)
        return st.T[0:qrows] + _dot(qp, pebufs[buf][...])

    next_first = jnp.where(last_step, 0, first_page + pages_per_step)

    def attend(s, buf, state):
        state = _online_step(s, kvbufs[buf][...], *state)
        start_chunk(next_first + buf * pages_per_chunk, buf)
        return state

    half_d = wdn_ref.shape[1] // 2

    wait_chunk(0)
    wait_chunk(1)
    s0 = scores(0)
    s1 = scores(1)
    state = attend(s0, 0, state)
    a = jnp.maximum(_dot(hf_ref[...], wup_ref[...]), 0.0)
    state = attend(s1, 1, state)
    a = (a * a).astype(BF16)

    wait_chunk(2)
    wait_chunk(3)
    s2 = scores(2)
    s3 = scores(3)
    state = attend(s2, 2, state)
    facc_sc[:, 0:half_d] += _dot(a, wdn_ref[:, 0:half_d])
    state = attend(s3, 3, state)
    facc_sc[:, half_d:] += _dot(a, wdn_ref[:, half_d:])
    m_sc[...], l_sc[...], acc_sc[...] = state

    @pl.when(last_step)
    def _():
        for buf in range(nbuf):
            wait_chunk(buf)

    @pl.when(c == steps_per_batch - 1)
    def _():
        o_ref[0] = (state[2] * (1.0 / state[1])).astype(BF16)

    @pl.when(f == nf - 1)
    def _():
        y_ref[...] = _rms(facc_sc[...], gfin_ref[...])


def _ffn_dattn(x2, hf, wup, wdn, gfin, page_table, q, knew, cache_kv, cache_pe, *, tm, tf, layer,
               pages_per_chunk, nbuf, new_tokens):
    rows, d = x2.shape
    d_ff = wup.shape[1]
    nbatch, qrows, _ = q.shape
    n_pages = page_table.shape[1]
    page = cache_kv.shape[2]
    keys = pages_per_chunk * page
    ni, nf = rows // tm, d_ff // tf
    steps_per_batch = n_pages // (nbuf * pages_per_chunk)
    assert ni * nf == nbatch * steps_per_batch
    row_spec = lambda width: pl.BlockSpec((tm, width), lambda i, f, pt: (i, 0))
    batch_spec = lambda r, width: pl.BlockSpec(
        (1, r, width), lambda i, f, pt: ((i * nf + f) // steps_per_batch, 0, 0))
    grid_spec = pltpu.PrefetchScalarGridSpec(
        num_scalar_prefetch=1, grid=(ni, nf),
        in_specs=[row_spec(d), row_spec(d),
                  pl.BlockSpec((d, tf), lambda i, f, pt: (0, f)),
                  pl.BlockSpec((tf, d), lambda i, f, pt: (f, 0)), _const_spec((1, d)),
                  batch_spec(qrows, KW), batch_spec(LANES, KW),
                  pl.BlockSpec(memory_space=pl.ANY), pl.BlockSpec(memory_space=pl.ANY)],
        out_specs=[row_spec(d), batch_spec(qrows, KV_RANK)],
        scratch_shapes=([pltpu.VMEM((tm, d), F32)]
                        + [pltpu.VMEM((keys, KV_RANK), F32)] * nbuf + [pltpu.VMEM((QK_ROPE, keys), F32)] * nbuf
                        + [pltpu.SemaphoreType.DMA((2, nbuf)),
                           pltpu.VMEM((qrows, 1), F32), pltpu.VMEM((qrows, 1), F32),
                           pltpu.VMEM((qrows, KV_RANK), F32)]))
    return pl.pallas_call(
        functools.partial(_ffn_dattn_kernel, layer=layer, pages_per_chunk=pages_per_chunk, nbuf=nbuf,
                          steps_per_batch=steps_per_batch, new_tokens=new_tokens),
        grid_spec=grid_spec,
        out_shape=[jax.ShapeDtypeStruct((rows, d), F32), jax.ShapeDtypeStruct((nbatch, qrows, KV_RANK), BF16)],
        compiler_params=_params(("arbitrary", "arbitrary")), name="ffn_dattn",
    )(page_table.reshape(-1), x2, hf, wup, wdn, gfin, q, knew, cache_kv, cache_pe)


def _merge_kernel(x_ref, o_ref, cm_ref, wuv_ref, ag_ref, wout_ref, gmem_ref, wmq_ref, *rest, fuse_mem):
    if fuse_mem:
        mk_ref, mv_ref, wmo_ref, gffn_ref, xo_ref, ho_ref, attn_sc, om_sc = rest
    else:
        xo_ref, ho_ref, attn_sc = rest
    for hh in range(HEADS):
        attn_sc[:, hh * V_HEAD:(hh + 1) * V_HEAD] = _dot(o_ref[hh], wuv_ref[hh])
    attn_n = _rms(attn_sc[...], ag_ref[...]).astype(BF16)
    mix = _dot(attn_n, wout_ref[0:AW, :]) + _dot(cm_ref[...], wout_ref[AW:AW + CW, :])
    x1 = x_ref[...] + mix
    qm = _dot(_rms(x1, gmem_ref[...]).astype(BF16), wmq_ref[...]) * MEM_SCALE
    if not fuse_mem:
        xo_ref[...] = x1
        ho_ref[...] = qm.astype(BF16)
        return
    for hh in range(MEM_HEADS):
        cs = slice(hh * MEM_HD, (hh + 1) * MEM_HD)
        s = _dot_nt(qm[:, cs].astype(BF16), mk_ref[0, :, cs])
        p = jnp.exp(s - jnp.max(s, axis=-1, keepdims=True))
        oh = _dot(p.astype(BF16), mv_ref[0, :, cs])
        om_sc[:, cs] = (oh * (1.0 / jnp.sum(p, axis=-1, keepdims=True))).astype(BF16)
    x2 = x1 + _dot(om_sc[...], wmo_ref[...])
    xo_ref[...] = x2
    ho_ref[...] = _rms(x2, gffn_ref[...]).astype(BF16)


def _merge(x, o_lat, cm, w, mem_kv, *, tm, tiles_per_batch):
    rows, d = x.shape
    fuse_mem = mem_kv is not None
    row_spec = lambda width: pl.BlockSpec((tm, width), lambda i: (i, 0))
    in_specs = [row_spec(d), pl.BlockSpec((HEADS, tm, KV_RANK), lambda i: (0, i, 0)), row_spec(CW),
                _const_spec((HEADS, KV_RANK, V_HEAD)), _const_spec((1, AW)),
                _const_spec((AW + CW, d)), _const_spec((1, d)), _const_spec((d, MEM_INNER))]
    args = [x, o_lat, cm, w["wuv"], w["ag"], w["wout"], w["gmem"], w["wmq"]]
    scratch = [pltpu.VMEM((tm, AW), F32)]
    second_width = MEM_INNER
    if fuse_mem:
        mem_len = mem_kv[0].shape[1]
        mem_spec = pl.BlockSpec((1, mem_len, MEM_INNER), lambda i: (i // tiles_per_batch, 0, 0))
        in_specs += [mem_spec, mem_spec, _const_spec((MEM_INNER, d)), _const_spec((1, d))]
        args += list(mem_kv) + [w["wmo"], w["gffn"]]
        scratch.append(pltpu.VMEM((tm, MEM_INNER), BF16))
        second_width = d
    return pl.pallas_call(
        functools.partial(_merge_kernel, fuse_mem=fuse_mem), grid=(rows // tm,),
        in_specs=in_specs, out_specs=[row_spec(d), row_spec(second_width)],
        out_shape=[jax.ShapeDtypeStruct((rows, d), F32), jax.ShapeDtypeStruct((rows, second_width), BF16)],
        scratch_shapes=scratch,
        compiler_params=_params(("arbitrary",)), name="merge_mem" if fuse_mem else "merge",
    )(*args)


def _smem_kernel(q_ref, mk_ref, mv_ref, o_ref):
    rows, mrows = q_ref.shape[1], mk_ref.shape[1]
    qhead = lax.broadcasted_iota(jnp.int32, (rows, mrows), 0) % MEM_HEADS
    mhead = lax.broadcasted_iota(jnp.int32, (rows, mrows), 1) % MEM_HEADS
    same_head = qhead == mhead

    def body(bi, carry):
        s = _dot_nt(q_ref[bi], mk_ref[bi].astype(BF16))
        s = jnp.where(same_head, s, -jnp.inf)
        p = jnp.exp(s - jnp.max(s, axis=-1, keepdims=True))
        o = _dot(p.astype(BF16), mv_ref[bi].astype(BF16))
        o_ref[bi] = (o * (1.0 / jnp.sum(p, axis=-1, keepdims=True))).astype(BF16)
        return carry
    lax.fori_loop(0, q_ref.shape[0], body, 0)


def _smem(q, mk, mv, *, layer, bb):
    nbatch, rows, _ = q.shape
    mrows = mk.shape[1]
    nblk = nbatch // bb
    q_spec = pl.BlockSpec((bb, rows, MEM_HD), lambda i: (i, 0, 0))
    m_spec = pl.BlockSpec((bb, mrows, MEM_HD), lambda i: (layer * nblk + i, 0, 0))
    return pl.pallas_call(
        _smem_kernel, grid=(nblk,), in_specs=[q_spec, m_spec, m_spec], out_specs=q_spec,
        out_shape=jax.ShapeDtypeStruct(q.shape, BF16),
        compiler_params=_params(("arbitrary",)), name="smem",
    )(q, mk, mv)


def _ffn_kernel(x1_ref, om_ref, wmo_ref, gffn_ref, wup_ref, wdn_ref, gfin_ref, y_ref, hf_sc, acc_sc):
    f = pl.program_id(1)

    @pl.when(f == 0)
    def _():
        x2 = x1_ref[...] + _dot(om_ref[...], wmo_ref[...])
        acc_sc[...] = x2
        hf_sc[...] = _rms(x2, gffn_ref[...]).astype(BF16)

    a = jnp.maximum(_dot(hf_sc[...], wup_ref[...]), 0.0)
    acc_sc[...] += _dot((a * a).astype(BF16), wdn_ref[...])

    @pl.when(f == pl.num_programs(1) - 1)
    def _():
        y_ref[...] = _rms(acc_sc[...], gfin_ref[...])


def _ffn(x1, om, w, *, tm, tf):
    rows, d = x1.shape
    d_ff = w["wup"].shape[1]
    row_spec = lambda width: pl.BlockSpec((tm, width), lambda i, f: (i, 0))
    return pl.pallas_call(
        _ffn_kernel, grid=(rows // tm, d_ff // tf),
        in_specs=[row_spec(d), row_spec(MEM_INNER), _const_spec((MEM_INNER, d)), _const_spec((1, d)),
                  pl.BlockSpec((d, tf), lambda i, f: (0, f)), pl.BlockSpec((tf, d), lambda i, f: (f, 0)),
                  _const_spec((1, d))],
        out_specs=row_spec(d), out_shape=jax.ShapeDtypeStruct((rows, d), F32),
        scratch_shapes=[pltpu.VMEM((tm, d), BF16), pltpu.VMEM((tm, d), F32)],
        compiler_params=_params(("arbitrary", "arbitrary")), name="ffn",
    )(x1, om, w["wmo"], w["gffn"], w["wup"], w["wdn"], w["gfin"])


def _rope_tables(pos):
    half = QK_ROPE // 2
    inv = ROPE_THETA ** (-jnp.arange(half, dtype=F32) / half)
    ang = pos.astype(F32)[:, None] * inv[None, :]
    cos, sin, z = jnp.cos(ang), jnp.sin(ang), jnp.zeros_like(ang)
    return (jnp.concatenate([cos, cos, z, z], axis=1),
            jnp.concatenate([-sin, z, z, z], axis=1),
            jnp.concatenate([z, sin, z, z], axis=1))


def _layer_weights(l, norm_mix_g, w_in, q_norm_g, w_q_up, kv_norm_g, w_uk, w_uv, v_norm_g, attn_out_g,
                   cmlp_out_g, w_out, norm_mem_g, w_mem_q, w_mem_o, norm_ffn_g, w_ffn_up, w_ffn_down,
                   final_norm_g):
    d = w_in.shape[1]
    o_pe_end = Q_RANK + KV_RANK + QK_ROPE
    win = jnp.concatenate([w_in[l, :, :o_pe_end], jnp.zeros((d, LANES - QK_ROPE), F32),
                           w_in[l, :, o_pe_end:]], axis=1).astype(BF16)
    wq = w_q_up[l].reshape(Q_RANK, HEADS, QK_NOPE + QK_ROPE)
    wq = jnp.concatenate([wq, jnp.zeros((Q_RANK, HEADS, 2 * LANES - QK_NOPE - QK_ROPE), F32)], axis=2)
    row = lambda g: g.reshape(1, -1)
    return dict(
        gmix=row(norm_mix_g[l]), win=win, qg=row(q_norm_g[l]),
        wq=wq.reshape(Q_RANK, HEADS * 2 * LANES).astype(BF16), kvg=row(kv_norm_g[l]),
        wuk=jnp.transpose(w_uk[l], (1, 2, 0)).astype(BF16), vg=row(v_norm_g[l]),
        cmg=row(cmlp_out_g[l]), wuv=jnp.transpose(w_uv[l], (1, 0, 2)).astype(BF16),
        ag=row(attn_out_g[l]), wout=w_out[l].astype(BF16), gmem=row(norm_mem_g[l]),
        wmq=w_mem_q[l].astype(BF16), wmo=w_mem_o[l].astype(BF16), gffn=row(norm_ffn_g[l]),
        wup32=w_ffn_up[l], wdn32=w_ffn_down[l], gfin=row(final_norm_g))


def kernel(x_prompt, x_sample, cache_kv_latent, cache_k_rope, cache_mem_k, cache_mem_v, page_table,
           mem_prompt, norm_mix_g, w_in, q_norm_g, w_q_up, kv_norm_g, w_uk, w_uv, v_norm_g, w_spatial,
           b_spatial, attn_out_g, cmlp_out_g, w_out, norm_mem_g, mem_in_g, w_mem_q, w_mem_k, w_mem_v,
           w_mem_o, norm_ffn_g, w_ffn_up, w_ffn_down, final_norm_g):
    batch, seq, d = x_prompt.shape
    dbatch, t_new, _ = x_sample.shape
    depth = w_in.shape[0]
    assert depth == 1 and seq % CHUNK == 0 and CHUNK % t_new == 0
    l = 0
    mem_len = mem_prompt.shape[1]
    tm = 256
    srows = dbatch * t_new

    w = _layer_weights(l, norm_mix_g, w_in, q_norm_g, w_q_up, kv_norm_g, w_uk, w_uv, v_norm_g,
                       attn_out_g, cmlp_out_g, w_out, norm_mem_g, w_mem_q, w_mem_o, norm_ffn_g,
                       w_ffn_up, w_ffn_down, final_norm_g)

    xp = x_prompt.reshape(batch * seq, d)
    tabs_p = _rope_tables(jnp.arange(seq))
    qt_p, kmat_p, kvc_p, kpe_p, cm_p, kvt_p, w["wup"], w["wdn"] = _proj(
        xp, tabs_p, seq // tm, w_spatial[l], b_spatial[l].T, w, (w["wup32"], w["wdn32"]),
        tm=tm, prompt=True)
    mk_p, mv_p = _memkv(mem_prompt.reshape(batch * mem_len, d), mem_in_g[l].reshape(1, -1),
                        w_mem_k[l].astype(BF16), w_mem_v[l].astype(BF16), tm=tm)
    o_p = _pattn(qt_p, kmat_p, kvt_p, batch=batch, seq=seq, tq=tm)
    mem_kv_p = (mk_p.astype(BF16).reshape(batch, mem_len, MEM_INNER),
                mv_p.astype(BF16).reshape(batch, mem_len, MEM_INNER))
    x2_p, hf_p = _merge(xp, o_p, cm_p, w, mem_kv_p, tm=tm, tiles_per_batch=seq // tm)

    xs = x_sample.reshape(srows, d)
    reps = CHUNK // t_new
    tabs_s = _rope_tables(PAST_LEN + jnp.arange(tm) % t_new)
    eye = jnp.eye(reps, dtype=F32)
    wmix_s = jnp.einsum("ab,gts->gatbs", eye, w_spatial[l][:, :t_new, :t_new]).reshape(GROUPS, CHUNK, CHUNK)
    bmix_s = jnp.tile(b_spatial[l][:, :t_new].T, (reps, 1))
    q_s, kmat_s, kvc_s, kpe_s, cm_s, v_s = _proj(xs, tabs_s, 1, wmix_s, bmix_s, w, tm=tm, prompt=False)

    q_sb = jnp.transpose(q_s.reshape(HEADS, dbatch, t_new, KW), (1, 0, 2, 3)).reshape(dbatch, HEADS * t_new, KW)
    knew = jnp.pad(kmat_s.reshape(dbatch, t_new, KW), ((0, 0), (0, LANES - t_new), (0, 0)))
    y_p, o_sb = _ffn_dattn(x2_p, hf_p, w["wup"], w["wdn"], w["gfin"], page_table, q_sb, knew,
                           cache_kv_latent, jnp.swapaxes(cache_k_rope, 2, 3), tm=512, tf=512, layer=l,
                           pages_per_chunk=16, nbuf=4, new_tokens=t_new)
    o_s = jnp.transpose(o_sb.reshape(dbatch, HEADS, t_new, KV_RANK), (1, 0, 2, 3)).reshape(HEADS, srows, KV_RANK)
    x1_s, qm_s = _merge(xs, o_s, cm_s, w, None, tm=tm, tiles_per_batch=1)
    mem_rows = (depth * dbatch, mem_len * MEM_HEADS, MEM_HD)
    om_s = _smem(qm_s.reshape(dbatch, t_new * MEM_HEADS, MEM_HD), cache_mem_k.reshape(mem_rows),
                 cache_mem_v.reshape(mem_rows), layer=l, bb=8)
    y_s = _ffn(x1_s, om_s.reshape(srows, MEM_INNER), w, tm=512, tf=1024)

    mem_shape = (depth, batch, mem_len, MEM_HEADS, MEM_HD)
    return (y_p.reshape(batch, seq, d), y_s.reshape(dbatch, t_new, d),
            kvc_p.reshape(depth, batch, seq, KV_RANK), kpe_p.reshape(depth, batch, seq, QK_ROPE),
            mk_p.reshape(mem_shape), mv_p.reshape(mem_shape),
            kvc_s.reshape(depth, dbatch, t_new, KV_RANK), kpe_s.reshape(depth, dbatch, t_new, QK_ROPE),
            v_s.reshape(depth, dbatch, t_new, GROUPS, GW))
```
